```python
import math
import jax, jax.numpy as jnp
from jax import lax
import numpy as np

D_MODEL = 4096
BATCH = 4
SEQ = 2048
DEPTH = 4
DEC_BATCH = 1
DEC_SEQ = 8192
PAST_LEN = 128

HEAD_DIM = 128
GROUP_HEADS = D_MODEL // HEAD_DIM // 4
A_HEADS = GROUP_HEADS
A_KV_HEADS = max(1, GROUP_HEADS // 4)
B_HEADS = GROUP_HEADS
B_KV_HEADS = max(1, GROUP_HEADS // 4)
C_HEADS = GROUP_HEADS
D_HEADS = GROUP_HEADS
D_SUB_DIM = HEAD_DIM // 2
MIX_WIDTH = (A_HEADS + B_HEADS + C_HEADS + D_HEADS) * HEAD_DIM
A_Q = A_HEADS * HEAD_DIM
A_KV = A_KV_HEADS * HEAD_DIM
B_Q = B_HEADS * HEAD_DIM
B_KV = B_KV_HEADS * HEAD_DIM
C_Q = C_HEADS * HEAD_DIM
D_W = D_HEADS * HEAD_DIM
IN_SIZES = (A_Q, A_KV, A_KV, B_Q, B_KV, B_KV, C_Q, C_Q, C_Q, D_W, D_W, D_W)
IN_WIDTH = sum(IN_SIZES)
D_FF = ((8 * D_MODEL // 3 + 255) // 256) * 256
Q_BLOCK = 128
WINDOW = 128
GRID_W = 64
NA_KH = 8
NA_KW = 16
ROPE_THETA = 10000.0
EPS = 1e-6
FFN_RESIDUAL = 0.5

kernel_name = 'hybrid_parallel_heads_encoder'

F32 = jnp.float32


def rms_norm(x, g):
    xf = x.astype(F32)
    y = xf * lax.rsqrt(jnp.mean(xf * xf, axis=-1, keepdims=True) + EPS)
    return (y * g.astype(F32)).astype(x.dtype)


def rope_angles(pos, dim):
    inv = ROPE_THETA ** (-jnp.arange(0, dim, 2, dtype=F32) / dim)
    ang = pos.astype(F32)[:, None] * inv[None, :]
    return jnp.cos(ang), jnp.sin(ang)


def apply_rope(x, cos, sin):
    half = x.shape[-1] // 2
    shape = (1, x.shape[1]) + (1,) * (x.ndim - 3) + (half,)
    c = cos.reshape(shape)
    sn = sin.reshape(shape)
    xf = x.astype(F32)
    x1, x2 = xf[..., :half], xf[..., half:]
    return jnp.concatenate([x1 * c - x2 * sn, x2 * c + x1 * sn], axis=-1).astype(x.dtype)


def apply_axial_rope(x, row, col):
    half = x.shape[-1] // 2
    cr, sr = rope_angles(row, half)
    cc, sc = rope_angles(col, half)
    return jnp.concatenate([apply_rope(x[..., :half], cr, sr), apply_rope(x[..., half:], cc, sc)], axis=-1)


def to_blocks(x, blk):
    b, s = x.shape[:2]
    return jnp.moveaxis(x.reshape((b, s // blk, blk) + x.shape[2:]), 1, 0)


def from_blocks(y):
    y = jnp.moveaxis(y, 0, 1)
    return y.reshape((y.shape[0], y.shape[1] * y.shape[2]) + y.shape[3:])


def global_axial_gqa(q, k, v, row, col):
    b, s = q.shape[:2]
    q = apply_axial_rope(q, row, col)
    k = apply_axial_rope(k, row, col)
    g = A_HEADS // A_KV_HEADS
    qg = q.reshape(b, s, A_KV_HEADS, g, HEAD_DIM)
    scale = HEAD_DIM ** -0.5

    def block(qb):
        logits = jnp.einsum('bqhgd,bkhd->bhgqk', qb, k, preferred_element_type=F32) * scale
        p = jax.nn.softmax(logits, axis=-1)
        return jnp.einsum('bhgqk,bkhd->bqhgd', p.astype(v.dtype), v)

    out = from_blocks(lax.map(block, to_blocks(qg, Q_BLOCK)))
    return out.reshape(b, s, A_HEADS * HEAD_DIM)


def window_gqa_sink(q, k, v, sink):
    b, s = q.shape[:2]
    nb = s // Q_BLOCK
    g = B_HEADS // B_KV_HEADS
    cos, sin = rope_angles(jnp.arange(s), HEAD_DIM)
    q = apply_rope(q, cos, sin)
    k = apply_rope(k, cos, sin)

    def bands(u):
        ub = jnp.pad(u, ((0, 0), (Q_BLOCK, Q_BLOCK), (0, 0), (0, 0)))
        ub = ub.reshape(b, nb + 2, Q_BLOCK, B_KV_HEADS, HEAD_DIM)
        return jnp.concatenate([ub[:, :-2], ub[:, 1:-1], ub[:, 2:]], axis=2)

    kb, vb = bands(k), bands(v)
    qb = q.reshape(b, nb, Q_BLOCK, B_KV_HEADS, g, HEAD_DIM)
    start = jnp.arange(nb)[:, None] * Q_BLOCK
    qpos = start + jnp.arange(Q_BLOCK)[None, :]
    kpos = start - Q_BLOCK + jnp.arange(3 * Q_BLOCK)[None, :]
    valid = ((jnp.abs(qpos[:, :, None] - kpos[:, None, :]) <= WINDOW)
             & (kpos[:, None, :] >= 0) & (kpos[:, None, :] < s))
    logits = jnp.einsum('bnqhgd,bnkhd->bnhgqk', qb, kb, preferred_element_type=F32) * (HEAD_DIM ** -0.5)
    logits = jnp.where(valid[None, :, None, None], logits, -jnp.inf)
    sink_col = jnp.broadcast_to(sink.astype(F32).reshape(1, 1, B_KV_HEADS, g, 1, 1), logits.shape[:-1] + (1,))
    p = jax.nn.softmax(jnp.concatenate([logits, sink_col], axis=-1), axis=-1)[..., :-1]
    out = jnp.einsum('bnhgqk,bnkhd->bnqhgd', p.astype(v.dtype), vb)
    return out.reshape(b, s, B_HEADS * HEAD_DIM)


def neighbourhood_attn(q, k, v, rpb):
    b, s = q.shape[:2]
    rows = s // GRID_W
    kh = min(NA_KH, rows)
    kw = NA_KW
    t = jnp.arange(s)
    r = t // GRID_W
    c = t % GRID_W
    rs = jnp.clip(r - kh // 2, 0, rows - kh)
    cs = jnp.clip(c - kw // 2, 0, GRID_W - kw)
    kr = rs[:, None, None] + jnp.arange(kh)[None, :, None]
    kc = cs[:, None, None] + jnp.arange(kw)[None, None, :]
    key_idx = (kr * GRID_W + kc).reshape(rows, GRID_W, kh * kw)
    bias_idx = ((kr - r[:, None, None] + NA_KH - 1) * (2 * NA_KW - 1)
                + (kc - c[:, None, None] + NA_KW - 1)).reshape(rows, GRID_W, kh * kw)
    rpb_flat = rpb.reshape(C_HEADS, -1).astype(F32)
    scale = HEAD_DIM ** -0.5

    def row_block(xs):
        qb, ki, bi = xs
        kg = jnp.take(k, ki, axis=1)
        vg = jnp.take(v, ki, axis=1)
        logits = jnp.einsum('bqhd,bqkhd->bhqk', qb, kg, preferred_element_type=F32) * scale
        logits = logits + rpb_flat[:, bi][None]
        p = jax.nn.softmax(logits, axis=-1)
        return jnp.einsum('bhqk,bqkhd->bqhd', p.astype(v.dtype), vg)

    out = from_blocks(lax.map(row_block, (to_blocks(q, GRID_W), key_idx, bias_idx)))
    return out.reshape(b, s, C_HEADS * HEAD_DIM)


def diff_attn(q, k, v, lq1, lk1, lq2, lk2, subln, lambda_init):
    b, s = q.shape[:2]
    cos, sin = rope_angles(jnp.arange(s), D_SUB_DIM)
    q = apply_rope(q, cos, sin)
    k = apply_rope(k, cos, sin)
    lam = (jnp.exp(jnp.sum(lq1.astype(F32) * lk1.astype(F32)))
           - jnp.exp(jnp.sum(lq2.astype(F32) * lk2.astype(F32))) + lambda_init)
    scale = D_SUB_DIM ** -0.5

    def block(qb):
        logits = jnp.einsum('bqhcd,bkhcd->bchqk', qb, k, preferred_element_type=F32) * scale
        p = jax.nn.softmax(logits, axis=-1)
        w = p[:, 0] - lam * p[:, 1]
        return jnp.einsum('bhqk,bkhd->bqhd', w.astype(v.dtype), v)

    out = from_blocks(lax.map(block, to_blocks(q, Q_BLOCK)))
    out = rms_norm(out, subln) * (1.0 - lambda_init)
    return out.reshape(b, s, D_HEADS * HEAD_DIM)


def swiglu(x, w_gate, w_up, w_down):
    return (jax.nn.silu(x @ w_gate) * (x @ w_up)) @ w_down


def token_mixing(h, p, l, lambda_init):
    b, s, _ = h.shape
    t = jnp.arange(s)
    row = t // GRID_W
    col = t % GRID_W
    proj = h @ p['w_in'][l]
    split_points = [int(v) for v in np.cumsum(IN_SIZES)[:-1]]
    parts = jnp.split(proj, split_points, axis=-1)

    def heads(u, n):
        return u.reshape(b, s, n, HEAD_DIM)

    qa = rms_norm(heads(parts[0], A_HEADS), p['q_norm_a'][l])
    ka = rms_norm(heads(parts[1], A_KV_HEADS), p['k_norm_a'][l])
    va = heads(parts[2], A_KV_HEADS)
    qb = rms_norm(heads(parts[3], B_HEADS), p['q_norm_b'][l])
    kb = rms_norm(heads(parts[4], B_KV_HEADS), p['k_norm_b'][l])
    vb = heads(parts[5], B_KV_HEADS)
    qc = rms_norm(heads(parts[6], C_HEADS), p['q_norm_c'][l])
    kc = rms_norm(heads(parts[7], C_HEADS), p['k_norm_c'][l])
    vc = heads(parts[8], C_HEADS)
    qd = rms_norm(parts[9].reshape(b, s, D_HEADS, 2, D_SUB_DIM), p['q_norm_d'][l])
    kd = rms_norm(parts[10].reshape(b, s, D_HEADS, 2, D_SUB_DIM), p['k_norm_d'][l])
    vd = heads(parts[11], D_HEADS)

    oa = global_axial_gqa(qa, ka, va, row, col)
    ob = window_gqa_sink(qb, kb, vb, p['sink_b'][l])
    oc = neighbourhood_attn(qc, kc, vc, p['rpb_c'][l])
    od = diff_attn(qd, kd, vd, p['lambda_q1'][l], p['lambda_k1'][l], p['lambda_q2'][l], p['lambda_k2'][l],
                   p['subln_d'][l], lambda_init)
    return jnp.concatenate([oa, ob, oc, od], axis=-1) @ p['w_out'][l]


def encoder_trunk(x, p):
    for l in range(DEPTH):
        lambda_init = 0.8 - 0.6 * math.exp(-0.3 * l)
        x = x + FFN_RESIDUAL * swiglu(rms_norm(x, p['norm_ffn1'][l]), p['w_ffn1_gate'][l], p['w_ffn1_up'][l], p['w_ffn1_down'][l])
        x = x + token_mixing(rms_norm(x, p['norm_mix'][l]), p, l, lambda_init)
        x = x + FFN_RESIDUAL * swiglu(rms_norm(x, p['norm_ffn2'][l]), p['w_ffn2_gate'][l], p['w_ffn2_up'][l], p['w_ffn2_down'][l])
    return x


def setup_inputs(seed: int = 0) -> dict:
    key = jax.random.key(seed)
    ks = jax.random.split(key, 30)

    def normal(i, shape, scale):
        return jax.random.normal(ks[i], shape, F32) * scale

    def gain(i, shape):
        return 1.0 + normal(i, shape, 0.02)

    return {
        'x_prompt': normal(0, (BATCH, SEQ, D_MODEL), 1.0),
        'x_sample': normal(1, (DEC_BATCH, DEC_SEQ, D_MODEL), 1.0),
        'norm_ffn1': gain(2, (DEPTH, D_MODEL)),
        'w_ffn1_gate': normal(3, (DEPTH, D_MODEL, D_FF), D_MODEL ** -0.5),
        'w_ffn1_up': normal(4, (DEPTH, D_MODEL, D_FF), D_MODEL ** -0.5),
        'w_ffn1_down': normal(5, (DEPTH, D_FF, D_MODEL), D_FF ** -0.5),
        'norm_mix': gain(6, (DEPTH, D_MODEL)),
        'w_in': normal(7, (DEPTH, D_MODEL, IN_WIDTH), D_MODEL ** -0.5),
        'q_norm_a': gain(8, (DEPTH, HEAD_DIM)),
        'k_norm_a': gain(9, (DEPTH, HEAD_DIM)),
        'q_norm_b': gain(10, (DEPTH, HEAD_DIM)),
        'k_norm_b': gain(11, (DEPTH, HEAD_DIM)),
        'sink_b': normal(12, (DEPTH, B_HEADS), 0.5),
        'q_norm_c': gain(13, (DEPTH, HEAD_DIM)),
        'k_norm_c': gain(14, (DEPTH, HEAD_DIM)),
        'rpb_c': normal(15, (DEPTH, C_HEADS, 2 * NA_KH - 1, 2 * NA_KW - 1), 0.1),
        'q_norm_d': gain(16, (DEPTH, D_SUB_DIM)),
        'k_norm_d': gain(17, (DEPTH, D_SUB_DIM)),
        'lambda_q1': normal(18, (DEPTH, D_SUB_DIM), 0.1),
        'lambda_k1': normal(19, (DEPTH, D_SUB_DIM), 0.1),
        'lambda_q2': normal(20, (DEPTH, D_SUB_DIM), 0.1),
        'lambda_k2': normal(21, (DEPTH, D_SUB_DIM), 0.1),
        'subln_d': gain(22, (DEPTH, HEAD_DIM)),
        'w_out': normal(23, (DEPTH, MIX_WIDTH, D_MODEL), MIX_WIDTH ** -0.5),
        'norm_ffn2': gain(24, (DEPTH, D_MODEL)),
        'w_ffn2_gate': normal(25, (DEPTH, D_MODEL, D_FF), D_MODEL ** -0.5),
        'w_ffn2_up': normal(26, (DEPTH, D_MODEL, D_FF), D_MODEL ** -0.5),
        'w_ffn2_down': normal(27, (DEPTH, D_FF, D_MODEL), D_FF ** -0.5),
    }


def reference(x_prompt, x_sample, norm_ffn1, w_ffn1_gate, w_ffn1_up, w_ffn1_down, norm_mix, w_in,
              q_norm_a, k_norm_a, q_norm_b, k_norm_b, sink_b, q_norm_c, k_norm_c, rpb_c,
              q_norm_d, k_norm_d, lambda_q1, lambda_k1, lambda_q2, lambda_k2, subln_d, w_out,
              norm_ffn2, w_ffn2_gate, w_ffn2_up, w_ffn2_down):
    params = dict(norm_ffn1=norm_ffn1, w_ffn1_gate=w_ffn1_gate, w_ffn1_up=w_ffn1_up, w_ffn1_down=w_ffn1_down,
                  norm_mix=norm_mix, w_in=w_in, q_norm_a=q_norm_a, k_norm_a=k_norm_a,
                  q_norm_b=q_norm_b, k_norm_b=k_norm_b, sink_b=sink_b,
                  q_norm_c=q_norm_c, k_norm_c=k_norm_c, rpb_c=rpb_c,
                  q_norm_d=q_norm_d, k_norm_d=k_norm_d, lambda_q1=lambda_q1, lambda_k1=lambda_k1,
                  lambda_q2=lambda_q2, lambda_k2=lambda_k2, subln_d=subln_d, w_out=w_out,
                  norm_ffn2=norm_ffn2, w_ffn2_gate=w_ffn2_gate, w_ffn2_up=w_ffn2_up, w_ffn2_down=w_ffn2_down)
    y_prompt = encoder_trunk(x_prompt, params)
    y_sample = encoder_trunk(x_sample, params)
    return (y_prompt, y_sample)
```

```python
import functools
import math

import numpy as np
import jax
import jax.numpy as jnp
from jax import lax
from jax.experimental import pallas as pl
from jax.experimental.pallas import tpu as pltpu

F32 = jnp.float32
BF16 = jnp.bfloat16

HEAD_DIM = 128
SUB_DIM = HEAD_DIM // 2
HEADS = 8
KV_HEADS = 2
GROUP = HEADS // KV_HEADS
GRID_W = 64
NA_KH = 8
NA_KW = 16
WINDOW = 128
ROPE_THETA = 10000.0
EPS = 1e-6
FFN_RESIDUAL = 0.5
MASK_VALUE = -1e30

QKV_BLOCKS = 72
A_Q, A_K, A_V = 0, 8, 10
B_Q, B_K, B_V = 12, 20, 22
C_Q, C_K, C_V = 24, 32, 40
D_Q, D_K, D_V = 48, 56, 64
MIX_Q = HEADS * HEAD_DIM

VMEM_CAP_BYTES = 60 * 1024 * 1024
MIB = 1024 * 1024


def _cparams(semantics, vmem_bytes):
    return pltpu.CompilerParams(
        dimension_semantics=semantics,
        vmem_limit_bytes=int(min(VMEM_CAP_BYTES, max(vmem_bytes, 16 * MIB))))


def _pick(n, candidates):
    for c in candidates:
        if n % c == 0:
            return c
    return n


def _rmsnorm_kernel(x_ref, g_ref, o_ref):
    x = x_ref[...]
    ms = jnp.mean(x * x, axis=-1, keepdims=True)
    o_ref[...] = (x * lax.rsqrt(ms + EPS) * g_ref[...]).astype(o_ref.dtype)


def _rmsnorm(x, g):
    t, d = x.shape
    tm = _pick(t, (256, 128, 64, 32, 16, 8))
    return pl.pallas_call(
        _rmsnorm_kernel,
        out_shape=jax.ShapeDtypeStruct((t, d), BF16),
        grid=(t // tm,),
        in_specs=[pl.BlockSpec((tm, d), lambda i: (i, 0)),
                  pl.BlockSpec((1, d), lambda i: (0, 0))],
        out_specs=pl.BlockSpec((tm, d), lambda i: (i, 0)),
        compiler_params=_cparams(("parallel",), 2 * tm * d * 6 + 4 * tm * d * 4),
        name="rmsnorm",
    )(x, g.reshape(1, d).astype(F32))


def _gateup_kernel(x_ref, wg_ref, wu_ref, o_ref):
    x = x_ref[...]
    g = jnp.dot(x, wg_ref[...], preferred_element_type=F32)
    u = jnp.dot(x, wu_ref[...], preferred_element_type=F32)
    o_ref[...] = (g * (1.0 / (1.0 + jnp.exp(-g))) * u).astype(o_ref.dtype)


def _gateup(xn, wg, wu):
    t, d = xn.shape
    f = wg.shape[1]
    tm = _pick(t, (1024, 512, 256, 128))
    tn = _pick(f, (512, 256, 128))
    vmem = 2 * (tm * d * 2 + 2 * d * tn * 2 + tm * tn * 2) + 8 * tm * tn * 4
    return pl.pallas_call(
        _gateup_kernel,
        out_shape=jax.ShapeDtypeStruct((t, f), BF16),
        grid=(t // tm, f // tn),
        in_specs=[pl.BlockSpec((tm, d), lambda i, j: (i, 0)),
                  pl.BlockSpec((d, tn), lambda i, j: (0, j)),
                  pl.BlockSpec((d, tn), lambda i, j: (0, j))],
        out_specs=pl.BlockSpec((tm, tn), lambda i, j: (i, j)),
        compiler_params=_cparams(("parallel", "arbitrary"), vmem),
        name="ffn_gateup",
    )(xn, wg, wu)


def _mm_res_kernel(a_ref, w_ref, r_ref, o_ref, *scratch, nk, scale):
    part = jnp.dot(a_ref[...], w_ref[...], preferred_element_type=F32)
    if nk == 1:
        o_ref[...] = r_ref[...] + scale * part
        return
    acc_ref, = scratch
    k = pl.program_id(2)

    @pl.when(k == 0)
    def _():
        acc_ref[...] = part

    @pl.when(jnp.logical_and(k > 0, k < nk - 1))
    def _():
        acc_ref[...] += part

    @pl.when(k == nk - 1)
    def _():
        o_ref[...] = r_ref[...] + scale * (acc_ref[...] + part)


def _mm_res(a, w, res, scale, *, tn_candidates, tk_candidates):
    t, kdim = a.shape
    n = w.shape[1]
    tm = _pick(t, (1024, 512, 256, 128))
    tn = _pick(n, tn_candidates)
    tk = _pick(kdim, tk_candidates)
    nk = kdim // tk
    scratch = [pltpu.VMEM((tm, tn), F32)] if nk > 1 else []
    vmem = 2 * (tm * tk * 2 + tk * tn * 2 + 2 * tm * tn * 4) + 3 * tm * tn * 4
    return pl.pallas_call(
        functools.partial(_mm_res_kernel, nk=nk, scale=scale),
        out_shape=jax.ShapeDtypeStruct((t, n), F32),
        grid=(t // tm, n // tn, nk),
        in_specs=[pl.BlockSpec((tm, tk), lambda i, j, k: (i, k)),
                  pl.BlockSpec((tk, tn), lambda i, j, k: (k, j)),
                  pl.BlockSpec((tm, tn), lambda i, j, k: (i, j))],
        out_specs=pl.BlockSpec((tm, tn), lambda i, j, k: (i, j)),
        scratch_shapes=scratch,
        compiler_params=_cparams(("parallel", "parallel", "arbitrary"), vmem),
        name="matmul_residual",
    )(a, w, res)


def _mm_kernel(a_ref, w_ref, o_ref):
    o_ref[...] = jnp.dot(a_ref[...], w_ref[...], preferred_element_type=F32)


def _mm(a, w):
    t, kdim = a.shape
    n = w.shape[1]
    tm = _pick(t, (1024, 512, 256, 128))
    tn = _pick(n, (512, 256, 128))
    vmem = 2 * (tm * kdim * 2 + kdim * tn * 2 + tm * tn * 4) + 2 * tm * tn * 4
    return pl.pallas_call(
        _mm_kernel,
        out_shape=jax.ShapeDtypeStruct((t, n), F32),
        grid=(t // tm, n // tn),
        in_specs=[pl.BlockSpec((tm, kdim), lambda i, j: (i, 0)),
                  pl.BlockSpec((kdim, tn), lambda i, j: (0, j))],
        out_specs=pl.BlockSpec((tm, tn), lambda i, j: (i, j)),
        compiler_params=_cparams(("parallel", "arbitrary"), vmem),
        name="in_proj",
    )(a, w)


_PREP_BLOCK_KINDS = (
    ("a", "a", "a", "a"), ("a", "a", "a", "a"), ("a", "a", "v", "v"),
    ("b", "b", "b", "b"), ("b", "b", "b", "b"), ("b", "b", "v", "v"),
    ("c", "c", "c", "c"), ("c", "c", "c", "c"), ("c", "c", "c", "c"),
    ("c", "c", "c", "c"), ("v", "v", "v", "v"), ("v", "v", "v", "v"),
    ("d", "d", "d", "d"), ("d", "d", "d", "d"), ("d", "d", "d", "d"),
    ("d", "d", "d", "d"), ("v", "v", "v", "v"), ("v", "v", "v", "v"),
)
_PREP_COLS = 4 * HEAD_DIM
_ROPE_OFF = {"a": 0, "b": 2 * HEAD_DIM, "d": 4 * HEAD_DIM}


def _prep_kernel(p_ref, g_ref, rope_ref, o_ref):
    j = pl.program_id(1)
    tm = p_ref.shape[0]
    lane = lax.broadcasted_iota(jnp.int32, (tm, HEAD_DIM), 1)
    low_half = lane < SUB_DIM
    low_quarter = (lane & (SUB_DIM - 1)) < SUB_DIM // 2

    def chunk(c, kind):
        cols = slice(c * HEAD_DIM, (c + 1) * HEAD_DIM)
        x = p_ref[:, cols]
        if kind == "v":
            o_ref[:, cols] = x.astype(o_ref.dtype)
            return
        sq = x * x
        if kind == "d":
            lo = jnp.sum(jnp.where(low_half, sq, 0.0), axis=-1, keepdims=True)
            hi = jnp.sum(jnp.where(low_half, 0.0, sq), axis=-1, keepdims=True)
            ms = jnp.where(low_half, lo, hi) * (1.0 / SUB_DIM)
        else:
            ms = jnp.mean(sq, axis=-1, keepdims=True)
        y = x * lax.rsqrt(ms + EPS) * g_ref[:, cols]
        if kind != "c":
            off = _ROPE_OFF[kind]
            cos = rope_ref[:, off:off + HEAD_DIM]
            sin = rope_ref[:, off + HEAD_DIM:off + 2 * HEAD_DIM]
            if kind == "b":
                partner = pltpu.roll(y, SUB_DIM, 1)
            else:
                partner = jnp.where(low_quarter,
                                    pltpu.roll(y, HEAD_DIM - SUB_DIM // 2, 1),
                                    pltpu.roll(y, SUB_DIM // 2, 1))
            y = y * cos + partner * sin
        o_ref[:, cols] = y.astype(o_ref.dtype)

    groups = {}
    for jj, kinds in enumerate(_PREP_BLOCK_KINDS):
        groups.setdefault(kinds, []).append(jj)
    for kinds, js in groups.items():
        cond = functools.reduce(jnp.logical_or, [j == jj for jj in js])

        @pl.when(cond)
        def _(kinds=kinds):
            for c, kind in enumerate(kinds):
                chunk(c, kind)


def _prep(proj, gains, rope):
    t, w = proj.shape
    assert w == QKV_BLOCKS * HEAD_DIM
    tm = _pick(t, (256, 128, 64, 32, 16))
    rw = rope.shape[1]
    vmem = 2 * (tm * _PREP_COLS * 6 + tm * rw * 4) + 8 * tm * _PREP_COLS * 4
    return pl.pallas_call(
        _prep_kernel,
        out_shape=jax.ShapeDtypeStruct((t, w), BF16),
        grid=(t // tm, w // _PREP_COLS),
        in_specs=[pl.BlockSpec((tm, _PREP_COLS), lambda i, j: (i, j)),
                  pl.BlockSpec((1, _PREP_COLS), lambda i, j: (0, j)),
                  pl.BlockSpec((tm, rw), lambda i, j: (i, 0))],
        out_specs=pl.BlockSpec((tm, _PREP_COLS), lambda i, j: (i, j)),
        compiler_params=_cparams(("parallel", "arbitrary"), vmem),
        name="qkv_norm_rope",
    )(proj, gains, rope)


def _flash_sweep(q_rows, k_ref, v_ref, m_scr, l_scr, acc_scr, *, seq, tk):
    m_scr[...] = jnp.full(m_scr.shape, -jnp.inf, F32)
    l_scr[...] = jnp.zeros(l_scr.shape, F32)
    acc_scr[...] = jnp.zeros(acc_scr.shape, F32)
    n_lane_blocks = tk // HEAD_DIM

    def body(c, carry):
        start = pl.multiple_of(c * tk, tk)
        k_c = k_ref[pl.ds(start, tk), :]
        v_c = v_ref[pl.ds(start, tk), :]
        s = lax.dot_general(q_rows, k_c, (((1,), (1,)), ((), ())),
                            preferred_element_type=F32)
        blocks = [s[:, i * HEAD_DIM:(i + 1) * HEAD_DIM] for i in range(n_lane_blocks)]
        m_prev = m_scr[...]
        m_cur = jnp.max(functools.reduce(jnp.maximum, blocks), axis=1, keepdims=True)
        m_new = jnp.maximum(m_prev, m_cur)
        alpha = jnp.exp(m_prev - m_new)
        p_blocks = [jnp.exp(b - m_new) for b in blocks]
        l_cur = jnp.sum(functools.reduce(jnp.add, p_blocks), axis=1, keepdims=True)
        p = jnp.concatenate(p_blocks, axis=1).astype(BF16)
        l_scr[...] = alpha * l_scr[...] + l_cur
        acc_scr[...] = alpha * acc_scr[...] + jnp.dot(p, v_c, preferred_element_type=F32)
        m_scr[...] = m_new
        return carry

    lax.fori_loop(0, seq // tk, body, 0)


def _flash_scratch(rows):
    return [pltpu.VMEM((rows, HEAD_DIM), F32) for _ in range(3)]


def _flash_vmem(seq, rows, tk, tq_cols):
    blocks = 2 * (2 * seq * HEAD_DIM * 2 + 2 * rows * HEAD_DIM * 2 * tq_cols)
    return blocks + 3 * rows * HEAD_DIM * 4 + 6 * rows * tk * 4


def _attn_a_kernel(q_ref, k_ref, v_ref, o_ref, m_scr, l_scr, acc_scr, *, seq, tq, tk):
    q = q_ref[...]
    q_rows = jnp.concatenate(
        [q[:, g * HEAD_DIM:(g + 1) * HEAD_DIM] for g in range(GROUP)], axis=0)
    _flash_sweep(q_rows, k_ref, v_ref, m_scr, l_scr, acc_scr, seq=seq, tk=tk)
    out = acc_scr[...] / l_scr[...]
    for g in range(GROUP):
        o_ref[:, g * HEAD_DIM:(g + 1) * HEAD_DIM] = (
            out[g * tq:(g + 1) * tq].astype(o_ref.dtype))


def _attn_a(qkv, *, seq, nseq, tok0):
    tq = 128
    tk = _pick(seq, (512, 256, 128))
    nq = seq // tq
    qb0, sb0 = tok0 // tq, tok0 // seq
    rows = GROUP * tq
    gw = GROUP * HEAD_DIM
    return pl.pallas_call(
        functools.partial(_attn_a_kernel, seq=seq, tq=tq, tk=tk),
        out_shape=jax.ShapeDtypeStruct((nseq * seq, MIX_Q), BF16),
        grid=(nseq, KV_HEADS, nq),
        in_specs=[
            pl.BlockSpec((tq, gw), lambda b, h, i: (qb0 + b * nq + i, A_Q // GROUP + h)),
            pl.BlockSpec((seq, HEAD_DIM), lambda b, h, i: (sb0 + b, A_K + h)),
            pl.BlockSpec((seq, HEAD_DIM), lambda b, h, i: (sb0 + b, A_V + h)),
        ],
        out_specs=pl.BlockSpec((tq, gw), lambda b, h, i: (b * nq + i, h)),
        scratch_shapes=_flash_scratch(rows),
        compiler_params=_cparams(("parallel", "parallel", "arbitrary"),
                                 _flash_vmem(seq, rows, tk, 1)),
        name="attn_global_gqa",
    )(qkv, qkv, qkv)


def _attn_d_kernel(q_ref, k_ref, v_ref, lam_ref, subln_ref, o_ref,
                   m_scr, l_scr, acc_scr, *, seq, tq, tk, lambda_init):
    q = q_ref[...]
    lane = lax.broadcasted_iota(jnp.int32, q.shape, 1)
    zero = jnp.zeros_like(q)
    q_rows = jnp.concatenate([jnp.where(lane < SUB_DIM, q, zero),
                              jnp.where(lane < SUB_DIM, zero, q)], axis=0)
    _flash_sweep(q_rows, k_ref, v_ref, m_scr, l_scr, acc_scr, seq=seq, tk=tk)
    out = acc_scr[...] / l_scr[...]
    lp = lam_ref[...]
    lam = (jnp.exp(jnp.sum(lp[0:1] * lp[1:2], axis=1, keepdims=True))
           - jnp.exp(jnp.sum(lp[2:3] * lp[3:4], axis=1, keepdims=True)) + lambda_init)
    d = out[:tq] - lam * out[tq:]
    ms = jnp.mean(d * d, axis=-1, keepdims=True)
    y = d * lax.rsqrt(ms + EPS) * subln_ref[...] * (1.0 - lambda_init)
    o_ref[...] = y.astype(o_ref.dtype)


def _attn_d(qkv, lam_params, subln, *, seq, nseq, tok0, lambda_init):
    tq = _pick(seq, (256, 128))
    tk = _pick(seq, (512, 256, 128))
    nq = seq // tq
    qb0, sb0 = tok0 // tq, tok0 // seq
    rows = 2 * tq
    return pl.pallas_call(
        functools.partial(_attn_d_kernel, seq=seq, tq=tq, tk=tk, lambda_init=lambda_init),
        out_shape=jax.ShapeDtypeStruct((nseq * seq, MIX_Q), BF16),
        grid=(nseq, HEADS, nq),
        in_specs=[
            pl.BlockSpec((tq, HEAD_DIM), lambda b, h, i: (qb0 + b * nq + i, D_Q + h)),
            pl.BlockSpec((seq, HEAD_DIM), lambda b, h, i: (sb0 + b, D_K + h)),
            pl.BlockSpec((seq, HEAD_DIM), lambda b, h, i: (sb0 + b, D_V + h)),
            pl.BlockSpec((4, SUB_DIM), lambda b, h, i: (0, 0)),
            pl.BlockSpec((1, HEAD_DIM), lambda b, h, i: (0, 0)),
        ],
        out_specs=pl.BlockSpec((tq, HEAD_DIM), lambda b, h, i: (b * nq + i, h)),
        scratch_shapes=_flash_scratch(rows),
        compiler_params=_cparams(("parallel", "parallel", "arbitrary"),
                                 _flash_vmem(seq, rows, tk, 1)),
        name="attn_differential",
    )(qkv, qkv, qkv, lam_params, subln)


_B_KEYS = 3 * WINDOW


def _attn_b_kernel(sink_ref, q_ref, k_ref, v_ref, o_ref, *, seq):
    h = pl.program_id(1)
    n = pl.program_id(2)
    tq = WINDOW
    kstart = pl.multiple_of(jnp.clip((n - 1) * tq, 0, seq - _B_KEYS), tq)
    k = k_ref[pl.ds(kstart, _B_KEYS), :]
    v = v_ref[pl.ds(kstart, _B_KEYS), :]
    q = q_ref[...]
    q_rows = jnp.concatenate(
        [q[:, g * HEAD_DIM:(g + 1) * HEAD_DIM] for g in range(GROUP)], axis=0)
    s = lax.dot_general(q_rows, k, (((1,), (1,)), ((), ())),
                        preferred_element_type=F32)
    rows = GROUP * tq
    qpos = n * tq + (lax.broadcasted_iota(jnp.int32, (rows, _B_KEYS), 0) & (tq - 1))
    kpos = kstart + lax.broadcasted_iota(jnp.int32, (rows, _B_KEYS), 1)
    s = jnp.where(jnp.abs(qpos - kpos) <= WINDOW, s, MASK_VALUE)
    sink = jnp.concatenate(
        [jnp.full((tq, 1), sink_ref[h * GROUP + g], F32) for g in range(GROUP)], axis=0)
    m = jnp.maximum(jnp.max(s, axis=1, keepdims=True), sink)
    p = jnp.exp(s - m)
    denom = jnp.sum(p, axis=1, keepdims=True) + jnp.exp(sink - m)
    out = jnp.dot(p.astype(BF16), v, preferred_element_type=F32) / denom
    for g in range(GROUP):
        o_ref[:, g * HEAD_DIM:(g + 1) * HEAD_DIM] = (
            out[g * tq:(g + 1) * tq].astype(o_ref.dtype))


def _attn_b(qkv, sink, *, seq, nseq, tok0):
    assert seq >= _B_KEYS
    tq = WINDOW
    nq = seq // tq
    qb0, sb0 = tok0 // tq, tok0 // seq
    gw = GROUP * HEAD_DIM
    vmem = 2 * (2 * seq * HEAD_DIM * 2 + 2 * tq * gw * 2) + 8 * GROUP * tq * _B_KEYS * 4
    return pl.pallas_call(
        functools.partial(_attn_b_kernel, seq=seq),
        out_shape=jax.ShapeDtypeStruct((nseq * seq, MIX_Q), BF16),
        grid_spec=pltpu.PrefetchScalarGridSpec(
            num_scalar_prefetch=1,
            grid=(nseq, KV_HEADS, nq),
            in_specs=[
                pl.BlockSpec((tq, gw), lambda b, h, i, s_: (qb0 + b * nq + i, B_Q // GROUP + h)),
                pl.BlockSpec((seq, HEAD_DIM), lambda b, h, i, s_: (sb0 + b, B_K + h)),
                pl.BlockSpec((seq, HEAD_DIM), lambda b, h, i, s_: (sb0 + b, B_V + h)),
            ],
            out_specs=pl.BlockSpec((tq, gw), lambda b, h, i, s_: (b * nq + i, h)),
        ),
        compiler_params=_cparams(("parallel", "parallel", "arbitrary"), vmem),
        name="attn_window_sink",
    )(sink, qkv, qkv, qkv)


_C_KEYS = NA_KH * GRID_W


def _attn_c_kernel(q_ref, k_ref, v_ref, bias_ref, o_ref, *, grid_rows, rows_per_step):
    i = pl.program_id(2)

    def body(rr, carry):
        r = i * rows_per_step + rr
        rs = jnp.clip(r - NA_KH // 2, 0, grid_rows - NA_KH)
        dr0 = rs - r + NA_KH - 1
        qstart = pl.multiple_of(rr * GRID_W, GRID_W)
        kstart = pl.multiple_of(rs * GRID_W, GRID_W)
        q = q_ref[pl.ds(qstart, GRID_W), :]
        k = k_ref[pl.ds(kstart, _C_KEYS), :]
        v = v_ref[pl.ds(kstart, _C_KEYS), :]
        s = lax.dot_general(q, k, (((1,), (1,)), ((), ())),
                            preferred_element_type=F32) + bias_ref[0, dr0]
        m = jnp.max(s, axis=1, keepdims=True)
        p = jnp.exp(s - m)
        denom = jnp.sum(p, axis=1, keepdims=True)
        out = jnp.dot(p.astype(BF16), v, preferred_element_type=F32) / denom
        o_ref[pl.ds(qstart, GRID_W), :] = out.astype(o_ref.dtype)
        return carry

    lax.fori_loop(0, rows_per_step, body, 0)


def _neighbourhood_bias(rpb):
    c = np.arange(GRID_W)[:, None]
    kc = np.arange(GRID_W)[None, :]
    cs = np.clip(c - NA_KW // 2, 0, GRID_W - NA_KW)
    inside = (kc >= cs) & (kc < cs + NA_KW)
    dc = np.clip(kc - c + NA_KW - 1, 0, 2 * NA_KW - 2)
    d0 = np.arange(NA_KH)[:, None] + np.arange(NA_KH)[None, :]
    g = rpb.astype(F32)[:, d0][:, :, :, dc]
    g = jnp.where(inside[None, None, None], g, MASK_VALUE)
    g = jnp.transpose(g, (0, 1, 3, 2, 4))
    return g.reshape(g.shape[0], NA_KH, GRID_W, _C_KEYS)


def _attn_c(qkv, bias, *, seq, nseq, tok0):
    grid_rows = seq // GRID_W
    assert grid_rows >= NA_KH
    tq = _pick(seq, (512, 256, 128))
    nq = seq // tq
    qb0, sb0 = tok0 // tq, tok0 // seq
    vmem = (2 * (2 * seq * HEAD_DIM * 2 + 2 * tq * HEAD_DIM * 2 + NA_KH * GRID_W * _C_KEYS * 4)
            + 8 * GRID_W * _C_KEYS * 4)
    return pl.pallas_call(
        functools.partial(_attn_c_kernel, grid_rows=grid_rows, rows_per_step=tq // GRID_W),
        out_shape=jax.ShapeDtypeStruct((nseq * seq, MIX_Q), BF16),
        grid=(nseq, HEADS, nq),
        in_specs=[
            pl.BlockSpec((tq, HEAD_DIM), lambda b, h, i: (qb0 + b * nq + i, C_Q + h)),
            pl.BlockSpec((seq, HEAD_DIM), lambda b, h, i: (sb0 + b, C_K + h)),
            pl.BlockSpec((seq, HEAD_DIM), lambda b, h, i: (sb0 + b, C_V + h)),
            pl.BlockSpec((1, NA_KH, GRID_W, _C_KEYS), lambda b, h, i: (h, 0, 0, 0)),
        ],
        out_specs=pl.BlockSpec((tq, HEAD_DIM), lambda b, h, i: (b * nq + i, h)),
        compiler_params=_cparams(("parallel", "parallel", "arbitrary"), vmem),
        name="attn_neighbourhood",
    )(qkv, qkv, qkv, bias)


def _rope_tables(seqs):
    pos = np.concatenate([np.tile(np.arange(s), n) for s, n in seqs]).astype(np.int32)
    pos = jnp.asarray(pos)
    row = (pos // GRID_W).astype(F32)[:, None]
    col = (pos % GRID_W).astype(F32)[:, None]
    t = pos.astype(F32)[:, None]
    inv64 = ROPE_THETA ** (-jnp.arange(0, SUB_DIM, 2, dtype=F32) / SUB_DIM)[None, :]
    inv128 = ROPE_THETA ** (-jnp.arange(0, HEAD_DIM, 2, dtype=F32) / HEAD_DIM)[None, :]

    def pair(ang):
        c, s = jnp.cos(ang), jnp.sin(ang)
        return jnp.concatenate([c, c], axis=1), jnp.concatenate([-s, s], axis=1)

    cr, sr = pair(row * inv64)
    cc, sc = pair(col * inv64)
    cb, sb = pair(t * inv128)
    cd, sd = pair(t * inv64)
    return jnp.concatenate([cr, cc, sr, sc, cb, sb, cd, cd, sd, sd], axis=1)


def _qkv_gains(p, l):
    scale = HEAD_DIM ** -0.5
    scale_d = SUB_DIM ** -0.5
    ones = lambda n: jnp.ones((n * HEAD_DIM,), F32)
    tile = lambda g, n: jnp.tile(g.astype(F32), n)
    parts = [
        tile(p["q_norm_a"][l], HEADS) * scale, tile(p["k_norm_a"][l], KV_HEADS), ones(KV_HEADS),
        tile(p["q_norm_b"][l], HEADS) * scale, tile(p["k_norm_b"][l], KV_HEADS), ones(KV_HEADS),
        tile(p["q_norm_c"][l], HEADS) * scale, tile(p["k_norm_c"][l], HEADS), ones(HEADS),
        tile(p["q_norm_d"][l], 2 * HEADS) * scale_d, tile(p["k_norm_d"][l], 2 * HEADS), ones(HEADS),
    ]
    return jnp.concatenate(parts).reshape(1, QKV_BLOCKS * HEAD_DIM)


def _pad_ffn(w_gate, w_up, w_down):
    f = w_gate.shape[-1]
    pad = -(-f // 512) * 512 - f
    wg = jnp.pad(w_gate.astype(BF16), ((0, 0), (0, pad)))
    wu = jnp.pad(w_up.astype(BF16), ((0, 0), (0, pad)))
    wd = jnp.pad(w_down.astype(BF16), ((0, pad), (0, 0)))
    return wg, wu, wd


def _ffn(x, g, wg, wu, wd):
    xn = _rmsnorm(x, g)
    hidden = _gateup(xn, wg, wu)
    return _mm_res(hidden, wd, x, FFN_RESIDUAL, tn_candidates=(1024, 512, 256, 128),
                   tk_candidates=(1408, 1024, 512, 256, 128))


def _token_mixing(x, p, l, seqs, rope, w_in, w_out):
    lambda_init = 0.8 - 0.6 * math.exp(-0.3 * l)
    xn = _rmsnorm(x, p["norm_mix"][l])
    proj = _mm(xn, w_in)
    qkv = _prep(proj, _qkv_gains(p, l), rope)
    bias = _neighbourhood_bias(p["rpb_c"][l])
    sink = p["sink_b"][l].astype(F32)
    lam_params = jnp.stack([p["lambda_q1"][l], p["lambda_k1"][l],
                            p["lambda_q2"][l], p["lambda_k2"][l]]).astype(F32)
    subln = p["subln_d"][l].astype(F32).reshape(1, HEAD_DIM)
    outs = []
    tok0 = 0
    for seq, nseq in seqs:
        kw = dict(seq=seq, nseq=nseq, tok0=tok0)
        outs.append(jnp.concatenate([
            _attn_a(qkv, **kw),
            _attn_b(qkv, sink, **kw),
            _attn_c(qkv, bias, **kw),
            _attn_d(qkv, lam_params, subln, lambda_init=lambda_init, **kw),
        ], axis=1))
        tok0 += seq * nseq
    mixed = jnp.concatenate(outs, axis=0)
    return _mm_res(mixed, w_out, x, 1.0, tn_candidates=(512, 256, 128),
                   tk_candidates=(mixed.shape[1],))


def kernel(x_prompt, x_sample, norm_ffn1, w_ffn1_gate, w_ffn1_up, w_ffn1_down, norm_mix, w_in,
           q_norm_a, k_norm_a, q_norm_b, k_norm_b, sink_b, q_norm_c, k_norm_c, rpb_c,
           q_norm_d, k_norm_d, lambda_q1, lambda_k1, lambda_q2, lambda_k2, subln_d, w_out,
           norm_ffn2, w_ffn2_gate, w_ffn2_up, w_ffn2_down):
    p = dict(norm_mix=norm_mix, q_norm_a=q_norm_a, k_norm_a=k_norm_a, q_norm_b=q_norm_b,
             k_norm_b=k_norm_b, sink_b=sink_b, q_norm_c=q_norm_c, k_norm_c=k_norm_c, rpb_c=rpb_c,
             q_norm_d=q_norm_d, k_norm_d=k_norm_d, lambda_q1=lambda_q1, lambda_k1=lambda_k1,
             lambda_q2=lambda_q2, lambda_k2=lambda_k2, subln_d=subln_d)
    depth = w_in.shape[0]
    d_model = x_prompt.shape[-1]
    bp, sp, _ = x_prompt.shape
    bs, ss, _ = x_sample.shape
    seqs = ((sp, bp), (ss, bs))
    assert (bp * sp) % ss == 0
    x = jnp.concatenate([x_prompt.reshape(bp * sp, d_model),
                         x_sample.reshape(bs * ss, d_model)], axis=0)
    rope = _rope_tables(seqs)
    for l in range(depth):
        x = _ffn(x, norm_ffn1[l], *_pad_ffn(w_ffn1_gate[l], w_ffn1_up[l], w_ffn1_down[l]))
        x = _token_mixing(x, p, l, seqs, rope, w_in[l].astype(BF16), w_out[l].astype(BF16))
        x = _ffn(x, norm_ffn2[l], *_pad_ffn(w_ffn2_gate[l], w_ffn2_up[l], w_ffn2_down[l]))
    y_prompt = x[:bp * sp].reshape(bp, sp, d_model)
    y_sample = x[bp * sp:].reshape(bs, ss, d_model)
    return (y_prompt, y_sample)
```

```python
import functools
import math

import numpy as np
import jax
import jax.numpy as jnp
from jax import lax
from jax.experimental import pallas as pl
from jax.experimental.pallas import tpu as pltpu

F32 = jnp.float32
BF16 = jnp.bfloat16

HEAD_DIM = 128
SUB_DIM = HEAD_DIM // 2
HEADS = 8
KV_HEADS = 2
GROUP = HEADS // KV_HEADS
GRID_W = 64
NA_KH = 8
NA_KW = 16
WINDOW = 128
ROPE_THETA = 10000.0
EPS = 1e-6
FFN_RESIDUAL = 0.5
MASK_VALUE = -1e30

QKV_BLOCKS = 72
A_Q, A_K, A_V = 0, 8, 10
B_Q, B_K, B_V = 12, 20, 22
C_Q, C_K, C_V = 24, 32, 40
D_Q, D_K, D_V = 48, 56, 64
MIX_Q = HEADS * HEAD_DIM

VMEM_CAP_BYTES = 60 * 1024 * 1024
MIB = 1024 * 1024
MXU_DIM = 256

FFN_TN = 512
KEY_CHUNK = 512


def _cparams(semantics, vmem_bytes):
    return pltpu.CompilerParams(
        dimension_semantics=semantics,
        vmem_limit_bytes=int(min(VMEM_CAP_BYTES, max(vmem_bytes, 16 * MIB))))


def _pick(n, candidates):
    for c in candidates:
        if n % c == 0:
            return c
    return n


def _round_up(n, m):
    return -(-n // m) * m


def _rmsnorm_kernel(x_ref, g_ref, o_ref):
    x = x_ref[...]
    ms = jnp.mean(x * x, axis=-1, keepdims=True)
    o_ref[...] = (x * lax.rsqrt(ms + EPS) * g_ref[...]).astype(o_ref.dtype)


def _rmsnorm(x, g):
    t, d = x.shape
    tm = _pick(t, (256, 128, 64, 32, 16, 8))
    return pl.pallas_call(
        _rmsnorm_kernel,
        out_shape=jax.ShapeDtypeStruct((t, d), BF16),
        grid=(t // tm,),
        in_specs=[pl.BlockSpec((tm, d), lambda i: (i, 0)),
                  pl.BlockSpec((1, d), lambda i: (0, 0))],
        out_specs=pl.BlockSpec((tm, d), lambda i: (i, 0)),
        compiler_params=_cparams(("parallel",), 2 * tm * d * 6 + 4 * tm * d * 4),
        name="rmsnorm",
    )(x, g.reshape(1, d).astype(F32))


def _gateup_kernel(x_ref, wg_ref, wu_ref, o_ref, *, d_ff):
    x = x_ref[...]
    g = jnp.dot(x, wg_ref[...], preferred_element_type=F32)
    u = jnp.dot(x, wu_ref[...], preferred_element_type=F32)
    h = g * (1.0 / (1.0 + jnp.exp(-g))) * u
    tn = o_ref.shape[1]
    if d_ff % tn:
        col = pl.program_id(1) * tn + lax.broadcasted_iota(jnp.int32, h.shape, 1)
        h = jnp.where(col < d_ff, h, 0.0)
    o_ref[...] = h.astype(o_ref.dtype)


def _gateup(xn, wg, wu, l):
    t, d = xn.shape
    d_ff = wg.shape[2]
    tm = _pick(t, (1024, 512, 256, 128))
    tn = FFN_TN
    nj = pl.cdiv(d_ff, tn)
    vmem = 2 * (tm * d * 2 + 2 * d * tn * 2 + tm * tn * 2) + 8 * tm * tn * 4
    return pl.pallas_call(
        functools.partial(_gateup_kernel, d_ff=d_ff),
        out_shape=jax.ShapeDtypeStruct((t, nj * tn), BF16),
        grid=(t // tm, nj),
        in_specs=[pl.BlockSpec((tm, d), lambda i, j: (i, 0)),
                  pl.BlockSpec((None, d, tn), lambda i, j: (l, 0, j)),
                  pl.BlockSpec((None, d, tn), lambda i, j: (l, 0, j))],
        out_specs=pl.BlockSpec((tm, tn), lambda i, j: (i, j)),
        compiler_params=_cparams(("parallel", "arbitrary"), vmem),
        name="ffn_gateup",
    )(xn, wg, wu)


def _mm_res_kernel(a_ref, w_ref, r_ref, o_ref, *scratch, nk, scale, k_valid):
    tk = w_ref.shape[0]

    def partial_sum(mask_tail):
        w = w_ref[...]
        if mask_tail:
            row = (nk - 1) * tk + lax.broadcasted_iota(jnp.int32, w.shape, 0)
            w = jnp.where(row < k_valid, w, jnp.zeros_like(w))
        return jnp.dot(a_ref[...], w, preferred_element_type=F32)

    ragged = k_valid < nk * tk
    if nk == 1:
        o_ref[...] = r_ref[...] + scale * partial_sum(ragged)
        return
    acc_ref, = scratch
    k = pl.program_id(2)

    @pl.when(k == 0)
    def _():
        acc_ref[...] = partial_sum(False)

    @pl.when(jnp.logical_and(k > 0, k < nk - 1))
    def _():
        acc_ref[...] += partial_sum(False)

    @pl.when(k == nk - 1)
    def _():
        o_ref[...] = r_ref[...] + scale * (acc_ref[...] + partial_sum(ragged))


def _mm_res(a, w, l, res, scale, *, tn_candidates, tk_candidates):
    t, ka = a.shape
    k_valid, n = w.shape[1], w.shape[2]
    tm = _pick(t, (1024, 512, 256, 128))
    tn = _pick(n, tn_candidates)
    tk = _pick(ka, tk_candidates)
    nk = ka // tk
    assert (nk - 1) * tk < k_valid <= ka
    scratch = [pltpu.VMEM((tm, tn), F32)] if nk > 1 else []
    vmem = 2 * (tm * tk * 2 + tk * tn * 2 + 2 * tm * tn * 4) + 3 * tm * tn * 4
    return pl.pallas_call(
        functools.partial(_mm_res_kernel, nk=nk, scale=scale, k_valid=k_valid),
        out_shape=jax.ShapeDtypeStruct((t, n), F32),
        grid=(t // tm, n // tn, nk),
        in_specs=[pl.BlockSpec((tm, tk), lambda i, j, k: (i, k)),
                  pl.BlockSpec((None, tk, tn), lambda i, j, k: (l, k, j)),
                  pl.BlockSpec((tm, tn), lambda i, j, k: (i, j))],
        out_specs=pl.BlockSpec((tm, tn), lambda i, j, k: (i, j)),
        scratch_shapes=scratch,
        compiler_params=_cparams(("parallel", "parallel", "arbitrary"), vmem),
        name="matmul_residual",
    )(a, w, res)


def _mm_kernel(a_ref, w_ref, o_ref):
    o_ref[...] = jnp.dot(a_ref[...], w_ref[...], preferred_element_type=F32)


def _mm(a, w, l):
    t, kdim = a.shape
    n = w.shape[2]
    tm = _pick(t, (1024, 512, 256, 128))
    tn = _pick(n, (512, 256, 128))
    vmem = 2 * (tm * kdim * 2 + kdim * tn * 2 + tm * tn * 4) + 2 * tm * tn * 4
    return pl.pallas_call(
        _mm_kernel,
        out_shape=jax.ShapeDtypeStruct((t, n), F32),
        grid=(t // tm, n // tn),
        in_specs=[pl.BlockSpec((tm, kdim), lambda i, j: (i, 0)),
                  pl.BlockSpec((None, kdim, tn), lambda i, j: (l, 0, j))],
        out_specs=pl.BlockSpec((tm, tn), lambda i, j: (i, j)),
        compiler_params=_cparams(("parallel", "arbitrary"), vmem),
        name="in_proj",
    )(a, w)


_PREP_BLOCK_KINDS = (
    ("a", "a", "a", "a"), ("a", "a", "a", "a"), ("a", "a", "v", "v"),
    ("b", "b", "b", "b"), ("b", "b", "b", "b"), ("b", "b", "v", "v"),
    ("c", "c", "c", "c"), ("c", "c", "c", "c"), ("c", "c", "c", "c"),
    ("c", "c", "c", "c"), ("v", "v", "v", "v"), ("v", "v", "v", "v"),
    ("d", "d", "d", "d"), ("d", "d", "d", "d"), ("d", "d", "d", "d"),
    ("d", "d", "d", "d"), ("v", "v", "v", "v"), ("v", "v", "v", "v"),
)
_PREP_COLS = 4 * HEAD_DIM
_ROPE_OFF = {"a": 0, "b": 2 * HEAD_DIM, "d": 4 * HEAD_DIM}


def _prep_kernel(p_ref, g_ref, rope_ref, o_ref):
    j = pl.program_id(1)
    tm = p_ref.shape[0]
    lane = lax.broadcasted_iota(jnp.int32, (tm, HEAD_DIM), 1)
    low_half = lane < SUB_DIM
    low_quarter = (lane & (SUB_DIM - 1)) < SUB_DIM // 2

    def chunk(c, kind):
        cols = slice(c * HEAD_DIM, (c + 1) * HEAD_DIM)
        x = p_ref[:, cols]
        if kind == "v":
            o_ref[:, cols] = x.astype(o_ref.dtype)
            return
        sq = x * x
        if kind == "d":
            lo = jnp.sum(jnp.where(low_half, sq, 0.0), axis=-1, keepdims=True)
            hi = jnp.sum(jnp.where(low_half, 0.0, sq), axis=-1, keepdims=True)
            ms = jnp.where(low_half, lo, hi) * (1.0 / SUB_DIM)
        else:
            ms = jnp.mean(sq, axis=-1, keepdims=True)
        y = x * lax.rsqrt(ms + EPS) * g_ref[:, cols]
        if kind != "c":
            off = _ROPE_OFF[kind]
            cos = rope_ref[:, off:off + HEAD_DIM]
            sin = rope_ref[:, off + HEAD_DIM:off + 2 * HEAD_DIM]
            if kind == "b":
                partner = pltpu.roll(y, SUB_DIM, 1)
            else:
                partner = jnp.where(low_quarter,
                                    pltpu.roll(y, HEAD_DIM - SUB_DIM // 2, 1),
                                    pltpu.roll(y, SUB_DIM // 2, 1))
            y = y * cos + partner * sin
        o_ref[:, cols] = y.astype(o_ref.dtype)

    groups = {}
    for jj, kinds in enumerate(_PREP_BLOCK_KINDS):
        groups.setdefault(kinds, []).append(jj)
    for kinds, js in groups.items():
        cond = functools.reduce(jnp.logical_or, [j == jj for jj in js])

        @pl.when(cond)
        def _(kinds=kinds):
            for c, kind in enumerate(kinds):
                chunk(c, kind)


def _prep(proj, gains, rope):
    t, w = proj.shape
    assert w == QKV_BLOCKS * HEAD_DIM
    tm = _pick(t, (512, 256, 128, 64, 32, 16))
    rw = rope.shape[1]
    vmem = 2 * (tm * _PREP_COLS * 6 + tm * rw * 4) + 8 * tm * _PREP_COLS * 4
    return pl.pallas_call(
        _prep_kernel,
        out_shape=jax.ShapeDtypeStruct((t, w), BF16),
        grid=(t // tm, w // _PREP_COLS),
        in_specs=[pl.BlockSpec((tm, _PREP_COLS), lambda i, j: (i, j)),
                  pl.BlockSpec((1, _PREP_COLS), lambda i, j: (0, j)),
                  pl.BlockSpec((tm, rw), lambda i, j: (i, 0))],
        out_specs=pl.BlockSpec((tm, _PREP_COLS), lambda i, j: (i, j)),
        compiler_params=_cparams(("parallel", "arbitrary"), vmem),
        name="qkv_norm_rope",
    )(proj, gains, rope)


_VT_HEADS = KV_HEADS + HEADS
_BF16_SUBLANES = 16
_VT_ROWS = HEAD_DIM + _BF16_SUBLANES


def _vt_kernel(p_ref, o_ref):
    o_ref[:HEAD_DIM, :] = p_ref[...].T.astype(o_ref.dtype)
    o_ref[HEAD_DIM:, :] = jnp.ones((_BF16_SUBLANES, o_ref.shape[1]), o_ref.dtype)


def _values_transposed(proj):
    t = proj.shape[0]
    tk = KEY_CHUNK
    return pl.pallas_call(
        _vt_kernel,
        out_shape=jax.ShapeDtypeStruct((_VT_HEADS, t // tk, _VT_ROWS, tk), BF16),
        grid=(_VT_HEADS, t // tk),
        in_specs=[pl.BlockSpec(
            (tk, HEAD_DIM),
            lambda h, i: (i, jnp.where(h < KV_HEADS, A_V + h, D_V - KV_HEADS + h)))],
        out_specs=pl.BlockSpec((None, None, _VT_ROWS, tk), lambda h, i: (h, i, 0, 0)),
        compiler_params=_cparams(("parallel", "parallel"), 16 * tk * HEAD_DIM * 4),
        name="values_transposed",
    )(proj)


def _pipelined(n, produce, consume, bufs, carry):
    if n == 1:
        produce(0, bufs[0])
        return consume(0, bufs[0], carry)
    assert n % 2 == 0
    produce(0, bufs[0])

    def body(c, carry):
        produce(2 * c + 1, bufs[1])
        carry = consume(2 * c, bufs[0], carry)
        produce(2 * c + 2, bufs[0])
        return consume(2 * c + 1, bufs[1], carry)

    carry = lax.fori_loop(0, n // 2 - 1, body, carry)
    produce(n - 1, bufs[1])
    carry = consume(n - 2, bufs[0], carry)
    return consume(n - 1, bufs[1], carry)


def _logits_scratch(rows, cols):
    return [pltpu.VMEM((rows, cols), F32), pltpu.VMEM((rows, cols), F32)]


def _flash_sweep(q_rows, k_ref, vt_ref, acc_scr, s_bufs, *, seq):
    rows = q_rows.shape[0]
    tk = KEY_CHUNK
    acc_scr[...] = jnp.zeros(acc_scr.shape, F32)

    def logits(chunk, dst):
        k_c = k_ref[pl.ds(pl.multiple_of(chunk * tk, tk), tk), :]
        dst[...] = lax.dot_general(k_c, q_rows, (((1,), (1,)), ((), ())),
                                   preferred_element_type=F32)

    def update(chunk, src, m):
        s = src[...]
        m_new = jnp.maximum(m, jnp.max(s, axis=0, keepdims=True))
        alpha = jnp.exp2(m - m_new)
        p = jnp.exp2(s - m_new).astype(BF16)
        pv = jnp.dot(vt_ref[chunk], p, preferred_element_type=F32)
        acc_scr[...] = alpha * acc_scr[...] + pv
        return m_new

    _pipelined(seq // tk, logits, update, s_bufs, jnp.full((1, rows), -jnp.inf, F32))
    return acc_scr[:HEAD_DIM, :] * (1.0 / acc_scr[HEAD_DIM:HEAD_DIM + 1, :])


def _flash_scratch(rows):
    return [pltpu.VMEM((_VT_ROWS, rows), F32)] + _logits_scratch(KEY_CHUNK, rows)


def _flash_vmem(seq, rows):
    blocks = 2 * (seq * HEAD_DIM * 2 + seq * _VT_ROWS * 2 + 2 * rows * HEAD_DIM * 2)
    return blocks + rows * _VT_ROWS * 4 + 12 * rows * KEY_CHUNK * 4


def _attn_a_kernel(q_ref, k_ref, vt_ref, o_ref, acc_scr, s0_scr, s1_scr, *, seq, tq):
    q = q_ref[...]
    q_rows = jnp.concatenate(
        [q[:, g * HEAD_DIM:(g + 1) * HEAD_DIM] for g in range(GROUP)], axis=0)
    out_t = _flash_sweep(q_rows, k_ref, vt_ref, acc_scr, (s0_scr, s1_scr), seq=seq)
    out = out_t.T
    for g in range(GROUP):
        o_ref[:, g * HEAD_DIM:(g + 1) * HEAD_DIM] = (
            out[g * tq:(g + 1) * tq].astype(o_ref.dtype))


def _attn_a(qkv, vt, *, seq, nseq, tok0):
    tq = 128
    nq = seq // tq
    nc = seq // KEY_CHUNK
    qb0, sb0 = tok0 // tq, tok0 // seq
    rows = GROUP * tq
    gw = GROUP * HEAD_DIM
    return pl.pallas_call(
        functools.partial(_attn_a_kernel, seq=seq, tq=tq),
        out_shape=jax.ShapeDtypeStruct((nseq * seq, MIX_Q), BF16),
        grid=(nseq, KV_HEADS, nq),
        in_specs=[
            pl.BlockSpec((tq, gw), lambda b, h, i: (qb0 + b * nq + i, A_Q // GROUP + h)),
            pl.BlockSpec((seq, HEAD_DIM), lambda b, h, i: (sb0 + b, A_K + h)),
            pl.BlockSpec((None, nc, _VT_ROWS, KEY_CHUNK), lambda b, h, i: (h, sb0 + b, 0, 0)),
        ],
        out_specs=pl.BlockSpec((tq, gw), lambda b, h, i: (b * nq + i, h)),
        scratch_shapes=_flash_scratch(rows),
        compiler_params=_cparams(("parallel", "parallel", "arbitrary"),
                                 _flash_vmem(seq, rows)),
        name="attn_global_gqa",
    )(qkv, qkv, vt)


def _attn_d_kernel(q_ref, k_ref, vt_ref, lam_ref, subln_ref, o_ref, acc_scr, s0_scr, s1_scr,
                   *, seq, tq, lambda_init):
    q = q_ref[...]
    lane = lax.broadcasted_iota(jnp.int32, q.shape, 1)
    zero = jnp.zeros_like(q)
    q_rows = jnp.concatenate([jnp.where(lane < SUB_DIM, q, zero),
                              jnp.where(lane < SUB_DIM, zero, q)], axis=0)
    out_t = _flash_sweep(q_rows, k_ref, vt_ref, acc_scr, (s0_scr, s1_scr), seq=seq)
    lp = lam_ref[...]
    lam = (jnp.exp(jnp.sum(lp[0:1] * lp[1:2], axis=1, keepdims=True))
           - jnp.exp(jnp.sum(lp[2:3] * lp[3:4], axis=1, keepdims=True)) + lambda_init)
    d = (out_t[:, :tq] - lam * out_t[:, tq:]).T
    ms = jnp.mean(d * d, axis=-1, keepdims=True)
    y = d * lax.rsqrt(ms + EPS) * subln_ref[...] * (1.0 - lambda_init)
    o_ref[...] = y.astype(o_ref.dtype)


def _attn_d(qkv, vt, lam_params, subln, *, seq, nseq, tok0, lambda_init):
    tq = _pick(seq, (256, 128))
    nq = seq // tq
    nc = seq // KEY_CHUNK
    qb0, sb0 = tok0 // tq, tok0 // seq
    rows = 2 * tq
    return pl.pallas_call(
        functools.partial(_attn_d_kernel, seq=seq, tq=tq, lambda_init=lambda_init),
        out_shape=jax.ShapeDtypeStruct((nseq * seq, MIX_Q), BF16),
        grid=(nseq, HEADS, nq),
        in_specs=[
            pl.BlockSpec((tq, HEAD_DIM), lambda b, h, i: (qb0 + b * nq + i, D_Q + h)),
            pl.BlockSpec((seq, HEAD_DIM), lambda b, h, i: (sb0 + b, D_K + h)),
            pl.BlockSpec((None, nc, _VT_ROWS, KEY_CHUNK),
                         lambda b, h, i: (KV_HEADS + h, sb0 + b, 0, 0)),
            pl.BlockSpec((4, SUB_DIM), lambda b, h, i: (0, 0)),
            pl.BlockSpec((1, HEAD_DIM), lambda b, h, i: (0, 0)),
        ],
        out_specs=pl.BlockSpec((tq, HEAD_DIM), lambda b, h, i: (b * nq + i, h)),
        scratch_shapes=_flash_scratch(rows),
        compiler_params=_cparams(("parallel", "parallel", "arbitrary"),
                                 _flash_vmem(seq, rows)),
        name="attn_differential",
    )(qkv, qkv, vt, lam_params, subln)


_B_KEYS = 3 * WINDOW


def _window_mask_table():
    o = np.arange(3)[:, None, None] * WINDOW
    r = np.arange(WINDOW)[None, :, None]
    c = np.arange(_B_KEYS)[None, None, :]
    return jnp.asarray(np.where(np.abs(o + r - c) <= WINDOW, 0.0, MASK_VALUE), F32)


def _attn_b_kernel(sink_ref, q_ref, k_ref, v_ref, mask_ref, o_ref, s0_scr, s1_scr,
                   *, seq, blocks_per_step):
    h = pl.program_id(1)
    step = pl.program_id(2)
    tq = WINDOW
    sink = jnp.concatenate(
        [jnp.full((tq, 1), sink_ref[h * GROUP + g], F32) for g in range(GROUP)], axis=0)

    def band_start(u):
        n = step * blocks_per_step + u
        kstart = jnp.clip((n - 1) * tq, 0, seq - _B_KEYS)
        return pl.multiple_of(kstart, tq), (n * tq - kstart) // tq

    def logits(u, dst):
        kstart, _ = band_start(u)
        q = q_ref[pl.ds(pl.multiple_of(u * tq, tq), tq), :]
        q_rows = jnp.concatenate(
            [q[:, g * HEAD_DIM:(g + 1) * HEAD_DIM] for g in range(GROUP)], axis=0)
        dst[...] = lax.dot_general(q_rows, k_ref[pl.ds(kstart, _B_KEYS), :],
                                   (((1,), (1,)), ((), ())),
                                   preferred_element_type=F32)

    def attend(u, src, carry):
        kstart, offset = band_start(u)
        mask = mask_ref[offset]
        s = src[...] + jnp.concatenate([mask] * GROUP, axis=0)
        m = jnp.maximum(jnp.max(s, axis=1, keepdims=True), sink)
        p = jnp.exp(s - m)
        denom = jnp.sum(p, axis=1, keepdims=True) + jnp.exp(sink - m)
        out = jnp.dot(p.astype(BF16), v_ref[pl.ds(kstart, _B_KEYS), :],
                      preferred_element_type=F32) / denom
        rows = pl.ds(pl.multiple_of(u * tq, tq), tq)
        for g in range(GROUP):
            o_ref[rows, g * HEAD_DIM:(g + 1) * HEAD_DIM] = (
                out[g * tq:(g + 1) * tq].astype(o_ref.dtype))
        return carry

    _pipelined(blocks_per_step, logits, attend, (s0_scr, s1_scr), 0)


def _attn_b(qkv, sink, *, seq, nseq, tok0):
    nblk = _pick(seq // WINDOW, (16, 8, 4, 2))
    ts = nblk * WINDOW
    assert seq >= _B_KEYS and seq % ts == 0
    nq = seq // ts
    qb0, sb0 = tok0 // ts, tok0 // seq
    gw = GROUP * HEAD_DIM
    rows = GROUP * WINDOW
    vmem = (2 * (2 * seq * HEAD_DIM * 2 + 2 * ts * gw * 2) + 3 * WINDOW * _B_KEYS * 4
            + 10 * rows * _B_KEYS * 4)
    return pl.pallas_call(
        functools.partial(_attn_b_kernel, seq=seq, blocks_per_step=nblk),
        out_shape=jax.ShapeDtypeStruct((nseq * seq, MIX_Q), BF16),
        grid_spec=pltpu.PrefetchScalarGridSpec(
            num_scalar_prefetch=1,
            grid=(nseq, KV_HEADS, nq),
            in_specs=[
                pl.BlockSpec((ts, gw), lambda b, h, i, s_: (qb0 + b * nq + i, B_Q // GROUP + h)),
                pl.BlockSpec((seq, HEAD_DIM), lambda b, h, i, s_: (sb0 + b, B_K + h)),
                pl.BlockSpec((seq, HEAD_DIM), lambda b, h, i, s_: (sb0 + b, B_V + h)),
                pl.BlockSpec((3, WINDOW, _B_KEYS), lambda b, h, i, s_: (0, 0, 0)),
            ],
            out_specs=pl.BlockSpec((ts, gw), lambda b, h, i, s_: (b * nq + i, h)),
            scratch_shapes=_logits_scratch(rows, _B_KEYS),
        ),
        compiler_params=_cparams(("parallel", "parallel", "arbitrary"), vmem),
        name="attn_window_sink",
    )(sink, qkv, qkv, qkv, _window_mask_table())


_C_KEYS = NA_KH * GRID_W
_C_BIAS_ROWS = 2 * NA_KH - 2
_C_ROWS_PER_STAGE = 4


def _attn_c_kernel(q_ref, k_ref, v_ref, bias_ref, o_ref, s0_scr, s1_scr,
                   *, grid_rows, rows_per_step):
    i = pl.program_id(2)
    stage_rows = _C_ROWS_PER_STAGE * GRID_W

    def window(st, t):
        r = i * rows_per_step + st * _C_ROWS_PER_STAGE + t
        rs = jnp.clip(r - NA_KH // 2, 0, grid_rows - NA_KH)
        return pl.multiple_of(rs * GRID_W, GRID_W), rs - r + NA_KH - 1

    def logits(st, dst):
        for t in range(_C_ROWS_PER_STAGE):
            kstart, _ = window(st, t)
            q = q_ref[pl.ds(pl.multiple_of(st * stage_rows + t * GRID_W, GRID_W), GRID_W), :]
            dst[t * GRID_W:(t + 1) * GRID_W, :] = lax.dot_general(
                q, k_ref[pl.ds(kstart, _C_KEYS), :], (((1,), (1,)), ((), ())),
                preferred_element_type=F32)

    def attend(st, src, carry):
        wins = [window(st, t) for t in range(_C_ROWS_PER_STAGE)]
        bias = jnp.concatenate(
            [jnp.concatenate([bias_ref[dr0 + 2 * u] for u in range(NA_KH // 2)], axis=1)
             for _, dr0 in wins], axis=0)
        s = src[...] + bias
        m = jnp.max(s, axis=1, keepdims=True)
        p = jnp.exp(s - m)
        denom = jnp.sum(p, axis=1, keepdims=True)
        pb = p.astype(BF16)
        out = jnp.concatenate(
            [jnp.dot(pb[t * GRID_W:(t + 1) * GRID_W], v_ref[pl.ds(kstart, _C_KEYS), :],
                     preferred_element_type=F32) for t, (kstart, _) in enumerate(wins)], axis=0)
        rows = pl.ds(pl.multiple_of(st * stage_rows, stage_rows), stage_rows)
        o_ref[rows, :] = (out / denom).astype(o_ref.dtype)
        return carry

    _pipelined(rows_per_step // _C_ROWS_PER_STAGE, logits, attend, (s0_scr, s1_scr), 0)


def _neighbourhood_bias(rpb):
    c = np.arange(GRID_W)[:, None]
    kc = np.arange(GRID_W)[None, :]
    cs = np.clip(c - NA_KW // 2, 0, GRID_W - NA_KW)
    inside = (kc >= cs) & (kc < cs + NA_KW)
    dc = np.clip(kc - c + NA_KW - 1, 0, 2 * NA_KW - 2)
    base = jnp.where(inside[None, None], rpb.astype(F32)[:, :, dc], MASK_VALUE)
    return jnp.concatenate([base[:, :-1], base[:, 1:]], axis=-1)


def _attn_c(qkv, bias, *, seq, nseq, tok0):
    grid_rows = seq // GRID_W
    assert grid_rows >= NA_KH
    rows_per_step = _pick(grid_rows, (32, 16, 8))
    tq = rows_per_step * GRID_W
    nq = seq // tq
    qb0, sb0 = tok0 // tq, tok0 // seq
    vmem = (2 * (2 * seq * HEAD_DIM * 2 + 2 * tq * HEAD_DIM * 2
                 + _C_BIAS_ROWS * GRID_W * 2 * GRID_W * 4)
            + 12 * _C_ROWS_PER_STAGE * GRID_W * _C_KEYS * 4)
    return pl.pallas_call(
        functools.partial(_attn_c_kernel, grid_rows=grid_rows, rows_per_step=rows_per_step),
        out_shape=jax.ShapeDtypeStruct((nseq * seq, MIX_Q), BF16),
        grid=(nseq, HEADS, nq),
        in_specs=[
            pl.BlockSpec((tq, HEAD_DIM), lambda b, h, i: (qb0 + b * nq + i, C_Q + h)),
            pl.BlockSpec((seq, HEAD_DIM), lambda b, h, i: (sb0 + b, C_K + h)),
            pl.BlockSpec((seq, HEAD_DIM), lambda b, h, i: (sb0 + b, C_V + h)),
            pl.BlockSpec((None, _C_BIAS_ROWS, GRID_W, 2 * GRID_W), lambda b, h, i: (h, 0, 0, 0)),
        ],
        out_specs=pl.BlockSpec((tq, HEAD_DIM), lambda b, h, i: (b * nq + i, h)),
        scratch_shapes=_logits_scratch(_C_ROWS_PER_STAGE * GRID_W, _C_KEYS),
        compiler_params=_cparams(("parallel", "parallel", "arbitrary"), vmem),
        name="attn_neighbourhood",
    )(qkv, qkv, qkv, bias)


def _rope_tables(seqs):
    pos = np.concatenate([np.tile(np.arange(s), n) for s, n in seqs]).astype(np.int32)
    pos = jnp.asarray(pos)
    row = (pos // GRID_W).astype(F32)[:, None]
    col = (pos % GRID_W).astype(F32)[:, None]
    t = pos.astype(F32)[:, None]
    inv64 = ROPE_THETA ** (-jnp.arange(0, SUB_DIM, 2, dtype=F32) / SUB_DIM)[None, :]
    inv128 = ROPE_THETA ** (-jnp.arange(0, HEAD_DIM, 2, dtype=F32) / HEAD_DIM)[None, :]

    def pair(ang):
        c, s = jnp.cos(ang), jnp.sin(ang)
        return jnp.concatenate([c, c], axis=1), jnp.concatenate([-s, s], axis=1)

    cr, sr = pair(row * inv64)
    cc, sc = pair(col * inv64)
    cb, sb = pair(t * inv128)
    cd, sd = pair(t * inv64)
    return jnp.concatenate([cr, cc, sr, sc, cb, sb, cd, cd, sd, sd], axis=1)


def _qkv_gains(p, l):
    scale = HEAD_DIM ** -0.5
    scale_d = SUB_DIM ** -0.5
    log2e = math.log2(math.e)
    ones = lambda n: jnp.ones((n * HEAD_DIM,), F32)
    tile = lambda g, n: jnp.tile(g.astype(F32), n)
    parts = [
        tile(p["q_norm_a"][l], HEADS) * (scale * log2e), tile(p["k_norm_a"][l], KV_HEADS),
        ones(KV_HEADS),
        tile(p["q_norm_b"][l], HEADS) * scale, tile(p["k_norm_b"][l], KV_HEADS), ones(KV_HEADS),
        tile(p["q_norm_c"][l], HEADS) * scale, tile(p["k_norm_c"][l], HEADS), ones(HEADS),
        tile(p["q_norm_d"][l], 2 * HEADS) * (scale_d * log2e), tile(p["k_norm_d"][l], 2 * HEADS),
        ones(HEADS),
    ]
    return jnp.concatenate(parts).reshape(1, QKV_BLOCKS * HEAD_DIM)


def _ffn(x, g, wg, wu, wd, l):
    xn = _rmsnorm(x, g)
    hidden = _gateup(xn, wg, wu, l)
    tk_candidates = tuple(m * MXU_DIM for m in (11, 8, 4, 2, 1))
    return _mm_res(hidden, wd, l, x, FFN_RESIDUAL, tn_candidates=(1024, 512, 256, 128),
                   tk_candidates=tk_candidates)


def _token_mixing(x, p, l, seqs, rope, w_in, w_out):
    lambda_init = 0.8 - 0.6 * math.exp(-0.3 * l)
    xn = _rmsnorm(x, p["norm_mix"][l])
    proj = _mm(xn, w_in, l)
    qkv = _prep(proj, _qkv_gains(p, l), rope)
    vt = _values_transposed(proj)
    bias = _neighbourhood_bias(p["rpb_c"][l])
    sink = p["sink_b"][l].astype(F32)
    lam_params = jnp.stack([p["lambda_q1"][l], p["lambda_k1"][l],
                            p["lambda_q2"][l], p["lambda_k2"][l]]).astype(F32)
    subln = p["subln_d"][l].astype(F32).reshape(1, HEAD_DIM)
    outs = []
    tok0 = 0
    for seq, nseq in seqs:
        kw = dict(seq=seq, nseq=nseq, tok0=tok0)
        outs.append(jnp.concatenate([
            _attn_a(qkv, vt, **kw),
            _attn_b(qkv, sink, **kw),
            _attn_c(qkv, bias, **kw),
            _attn_d(qkv, vt, lam_params, subln, lambda_init=lambda_init, **kw),
        ], axis=1))
        tok0 += seq * nseq
    mixed = jnp.concatenate(outs, axis=0)
    return _mm_res(mixed, w_out, l, x, 1.0, tn_candidates=(512, 256, 128),
                   tk_candidates=(mixed.shape[1],))


def kernel(x_prompt, x_sample, norm_ffn1, w_ffn1_gate, w_ffn1_up, w_ffn1_down, norm_mix, w_in,
           q_norm_a, k_norm_a, q_norm_b, k_norm_b, sink_b, q_norm_c, k_norm_c, rpb_c,
           q_norm_d, k_norm_d, lambda_q1, lambda_k1, lambda_q2, lambda_k2, subln_d, w_out,
           norm_ffn2, w_ffn2_gate, w_ffn2_up, w_ffn2_down):
    p = dict(norm_mix=norm_mix, q_norm_a=q_norm_a, k_norm_a=k_norm_a, q_norm_b=q_norm_b,
             k_norm_b=k_norm_b, sink_b=sink_b, q_norm_c=q_norm_c, k_norm_c=k_norm_c, rpb_c=rpb_c,
             q_norm_d=q_norm_d, k_norm_d=k_norm_d, lambda_q1=lambda_q1, lambda_k1=lambda_k1,
             lambda_q2=lambda_q2, lambda_k2=lambda_k2, subln_d=subln_d)
    depth = w_in.shape[0]
    d_model = x_prompt.shape[-1]
    bp, sp, _ = x_prompt.shape
    bs, ss, _ = x_sample.shape
    seqs = ((sp, bp), (ss, bs))
    assert (bp * sp) % ss == 0
    x = jnp.concatenate([x_prompt.reshape(bp * sp, d_model),
                         x_sample.reshape(bs * ss, d_model)], axis=0)
    rope = _rope_tables(seqs)
    ffn1 = [w.astype(BF16) for w in (w_ffn1_gate, w_ffn1_up, w_ffn1_down)]
    ffn2 = [w.astype(BF16) for w in (w_ffn2_gate, w_ffn2_up, w_ffn2_down)]
    w_in_b = w_in.astype(BF16)
    w_out_b = w_out.astype(BF16)
    for l in range(depth):
        x = _ffn(x, norm_ffn1[l], *ffn1, l)
        x = _token_mixing(x, p, l, seqs, rope, w_in_b, w_out_b)
        x = _ffn(x, norm_ffn2[l], *ffn2, l)
    y_prompt = x[:bp * sp].reshape(bp, sp, d_model)
    y_sample = x[bp * sp:].reshape(bs, ss, d_model)
    return (y_prompt, y_sample)
```

```python
import functools
import math

import numpy as np
import jax
import jax.numpy as jnp
from jax import lax
from jax.experimental import pallas as pl
from jax.experimental.pallas import tpu as pltpu

F32 = jnp.float32
BF16 = jnp.bfloat16

HEAD_DIM = 128
SUB_DIM = HEAD_DIM // 2
HEADS = 8
KV_HEADS = 2
GROUP = HEADS // KV_HEADS
GRID_W = 64
NA_KH = 8
NA_KW = 16
WINDOW = 128
ROPE_THETA = 10000.0
EPS = 1e-6
FFN_RESIDUAL = 0.5
MASK_VALUE = -1e30

QKV_BLOCKS = 72
A_Q, A_K, A_V = 0, 8, 10
B_Q, B_K, B_V = 12, 20, 22
C_Q, C_K, C_V = 24, 32, 40
D_Q, D_K, D_V = 48, 56, 64
MIX_Q = HEADS * HEAD_DIM

VMEM_CAP_BYTES = 60 * 1024 * 1024
MIB = 1024 * 1024
MXU_DIM = 256

FFN_TN = 512
KEY_CHUNK = 512
FLASH_ITEMS_PER_STEP = 64


def _cparams(semantics, vmem_bytes):
    return pltpu.CompilerParams(
        dimension_semantics=semantics,
        vmem_limit_bytes=int(min(VMEM_CAP_BYTES, max(vmem_bytes, 16 * MIB))))


def _pick(n, candidates):
    for c in candidates:
        if n % c == 0:
            return c
    return n


def _rmsnorm_kernel(x_ref, g_ref, o_ref):
    x = x_ref[...]
    ms = jnp.mean(x * x, axis=-1, keepdims=True)
    o_ref[...] = (x * lax.rsqrt(ms + EPS) * g_ref[...]).astype(o_ref.dtype)


def _rmsnorm(x, g):
    t, d = x.shape
    tm = _pick(t, (256, 128, 64, 32, 16, 8))
    return pl.pallas_call(
        _rmsnorm_kernel,
        out_shape=jax.ShapeDtypeStruct((t, d), BF16),
        grid=(t // tm,),
        in_specs=[pl.BlockSpec((tm, d), lambda i: (i, 0)),
                  pl.BlockSpec((1, d), lambda i: (0, 0))],
        out_specs=pl.BlockSpec((tm, d), lambda i: (i, 0)),
        compiler_params=_cparams(("parallel",), 2 * tm * d * 6 + 4 * tm * d * 4),
        name="rmsnorm",
    )(x, g.reshape(1, d).astype(F32))


def _gateup_kernel(x_ref, wg_ref, wu_ref, o_ref, *, d_ff):
    x = x_ref[...]
    g = jnp.dot(x, wg_ref[...], preferred_element_type=F32)
    u = jnp.dot(x, wu_ref[...], preferred_element_type=F32)
    h = g * (1.0 / (1.0 + jnp.exp(-g))) * u
    tn = o_ref.shape[1]
    if d_ff % tn:
        col = pl.program_id(1) * tn + lax.broadcasted_iota(jnp.int32, h.shape, 1)
        h = jnp.where(col < d_ff, h, 0.0)
    o_ref[...] = h.astype(o_ref.dtype)


def _gateup(xn, wg, wu, l):
    t, d = xn.shape
    d_ff = wg.shape[2]
    tm = _pick(t, (1024, 512, 256, 128))
    tn = FFN_TN
    nj = pl.cdiv(d_ff, tn)
    vmem = 2 * (tm * d * 2 + 2 * d * tn * 2 + tm * tn * 2) + 8 * tm * tn * 4
    return pl.pallas_call(
        functools.partial(_gateup_kernel, d_ff=d_ff),
        out_shape=jax.ShapeDtypeStruct((t, nj * tn), BF16),
        grid=(t // tm, nj),
        in_specs=[pl.BlockSpec((tm, d), lambda i, j: (i, 0)),
                  pl.BlockSpec((None, d, tn), lambda i, j: (l, 0, j)),
                  pl.BlockSpec((None, d, tn), lambda i, j: (l, 0, j))],
        out_specs=pl.BlockSpec((tm, tn), lambda i, j: (i, j)),
        compiler_params=_cparams(("parallel", "arbitrary"), vmem),
        name="ffn_gateup",
    )(xn, wg, wu)


def _mm_res_kernel(a_ref, w_ref, r_ref, o_ref, *scratch, nk, scale, k_valid):
    tk = w_ref.shape[0]

    def partial_sum(mask_tail):
        w = w_ref[...]
        if mask_tail:
            row = (nk - 1) * tk + lax.broadcasted_iota(jnp.int32, w.shape, 0)
            w = jnp.where(row < k_valid, w, jnp.zeros_like(w))
        return jnp.dot(a_ref[...], w, preferred_element_type=F32)

    ragged = k_valid < nk * tk
    if nk == 1:
        o_ref[...] = r_ref[...] + scale * partial_sum(ragged)
        return
    acc_ref, = scratch
    k = pl.program_id(2)

    @pl.when(k == 0)
    def _():
        acc_ref[...] = partial_sum(False)

    @pl.when(jnp.logical_and(k > 0, k < nk - 1))
    def _():
        acc_ref[...] += partial_sum(False)

    @pl.when(k == nk - 1)
    def _():
        o_ref[...] = r_ref[...] + scale * (acc_ref[...] + partial_sum(ragged))


def _mm_res(a, w, l, res, scale, *, tn_candidates, tk_candidates):
    t, ka = a.shape
    k_valid, n = w.shape[1], w.shape[2]
    tm = _pick(t, (1024, 512, 256, 128))
    tn = _pick(n, tn_candidates)
    tk = _pick(ka, tk_candidates)
    nk = ka // tk
    assert (nk - 1) * tk < k_valid <= ka
    scratch = [pltpu.VMEM((tm, tn), F32)] if nk > 1 else []
    vmem = 2 * (tm * tk * 2 + tk * tn * 2 + 2 * tm * tn * 4) + 3 * tm * tn * 4
    return pl.pallas_call(
        functools.partial(_mm_res_kernel, nk=nk, scale=scale, k_valid=k_valid),
        out_shape=jax.ShapeDtypeStruct((t, n), F32),
        grid=(t // tm, n // tn, nk),
        in_specs=[pl.BlockSpec((tm, tk), lambda i, j, k: (i, k)),
                  pl.BlockSpec((None, tk, tn), lambda i, j, k: (l, k, j)),
                  pl.BlockSpec((tm, tn), lambda i, j, k: (i, j))],
        out_specs=pl.BlockSpec((tm, tn), lambda i, j, k: (i, j)),
        scratch_shapes=scratch,
        compiler_params=_cparams(("parallel", "parallel", "arbitrary"), vmem),
        name="matmul_residual",
    )(a, w, res)


def _mm_kernel(a_ref, w_ref, o_ref):
    o_ref[...] = jnp.dot(a_ref[...], w_ref[...], preferred_element_type=F32)


def _mm(a, w, l):
    t, kdim = a.shape
    n = w.shape[2]
    tm = _pick(t, (1024, 512, 256, 128))
    tn = _pick(n, (512, 256, 128))
    vmem = 2 * (tm * kdim * 2 + kdim * tn * 2 + tm * tn * 4) + 2 * tm * tn * 4
    return pl.pallas_call(
        _mm_kernel,
        out_shape=jax.ShapeDtypeStruct((t, n), F32),
        grid=(t // tm, n // tn),
        in_specs=[pl.BlockSpec((tm, kdim), lambda i, j: (i, 0)),
                  pl.BlockSpec((None, kdim, tn), lambda i, j: (l, 0, j))],
        out_specs=pl.BlockSpec((tm, tn), lambda i, j: (i, j)),
        compiler_params=_cparams(("parallel", "arbitrary"), vmem),
        name="in_proj",
    )(a, w)


_PREP_CHUNK_KINDS = (
    "a" * (HEADS + KV_HEADS) + "v" * KV_HEADS + "b" * (HEADS + KV_HEADS) + "v" * KV_HEADS
    + "c" * (2 * HEADS) + "v" * HEADS + "d" * (2 * HEADS) + "v" * HEADS)
_PREP_CHUNKS = 8
_PREP_COLS = _PREP_CHUNKS * HEAD_DIM
_PREP_BLOCK_KINDS = tuple(
    tuple(_PREP_CHUNK_KINDS[i:i + _PREP_CHUNKS]) for i in range(0, QKV_BLOCKS, _PREP_CHUNKS))
_ROPE_OFF = {"a": 0, "b": 2 * HEAD_DIM, "d": 4 * HEAD_DIM}


def _prep_kernel(p_ref, g_ref, rope_ref, o_ref):
    j = pl.program_id(1)
    tm = p_ref.shape[0]
    lane = lax.broadcasted_iota(jnp.int32, (tm, HEAD_DIM), 1)
    low_half = lane < SUB_DIM
    low_quarter = (lane & (SUB_DIM - 1)) < SUB_DIM // 2

    def chunk(c, kind):
        cols = slice(c * HEAD_DIM, (c + 1) * HEAD_DIM)
        x = p_ref[:, cols]
        if kind == "v":
            o_ref[:, cols] = x.astype(o_ref.dtype)
            return
        sq = x * x
        if kind == "d":
            lo = jnp.sum(jnp.where(low_half, sq, 0.0), axis=-1, keepdims=True)
            hi = jnp.sum(jnp.where(low_half, 0.0, sq), axis=-1, keepdims=True)
            ms = jnp.where(low_half, lo, hi) * (1.0 / SUB_DIM)
        else:
            ms = jnp.mean(sq, axis=-1, keepdims=True)
        y = x * lax.rsqrt(ms + EPS) * g_ref[:, cols]
        if kind != "c":
            off = _ROPE_OFF[kind]
            cos = rope_ref[:, off:off + HEAD_DIM]
            sin = rope_ref[:, off + HEAD_DIM:off + 2 * HEAD_DIM]
            if kind == "b":
                partner = pltpu.roll(y, SUB_DIM, 1)
            else:
                partner = jnp.where(low_quarter,
                                    pltpu.roll(y, HEAD_DIM - SUB_DIM // 2, 1),
                                    pltpu.roll(y, SUB_DIM // 2, 1))
            y = y * cos + partner * sin
        o_ref[:, cols] = y.astype(o_ref.dtype)

    groups = {}
    for jj, kinds in enumerate(_PREP_BLOCK_KINDS):
        groups.setdefault(kinds, []).append(jj)
    for kinds, js in groups.items():
        cond = functools.reduce(jnp.logical_or, [j == jj for jj in js])

        @pl.when(cond)
        def _(kinds=kinds):
            for c, kind in enumerate(kinds):
                chunk(c, kind)


def _prep(proj, gains, rope):
    t, w = proj.shape
    assert w == QKV_BLOCKS * HEAD_DIM
    tm = _pick(t, (512, 256, 128, 64, 32, 16))
    rw = rope.shape[1]
    vmem = 2 * (tm * _PREP_COLS * 6 + tm * rw * 4) + 8 * tm * _PREP_COLS * 4
    return pl.pallas_call(
        _prep_kernel,
        out_shape=jax.ShapeDtypeStruct((t, w), BF16),
        grid=(t // tm, w // _PREP_COLS),
        in_specs=[pl.BlockSpec((tm, _PREP_COLS), lambda i, j: (i, j)),
                  pl.BlockSpec((1, _PREP_COLS), lambda i, j: (0, j)),
                  pl.BlockSpec((tm, rw), lambda i, j: (i, 0))],
        out_specs=pl.BlockSpec((tm, _PREP_COLS), lambda i, j: (i, j)),
        compiler_params=_cparams(("parallel", "arbitrary"), vmem),
        name="qkv_norm_rope",
    )(proj, gains, rope)


_VT_HEADS = KV_HEADS + HEADS
_BF16_SUBLANES = 16
_VT_ROWS = HEAD_DIM + _BF16_SUBLANES


def _vt_kernel(va_ref, vd_ref, o_ref):
    ones = jnp.ones((_BF16_SUBLANES, o_ref.shape[-1]), o_ref.dtype)
    heads = ([(va_ref, h) for h in range(KV_HEADS)] + [(vd_ref, h) for h in range(HEADS)])
    for slot, (ref, h) in enumerate(heads):
        o_ref[slot, :HEAD_DIM, :] = ref[:, h * HEAD_DIM:(h + 1) * HEAD_DIM].T.astype(o_ref.dtype)
        o_ref[slot, HEAD_DIM:, :] = ones


def _values_transposed(proj):
    t = proj.shape[0]
    tk = KEY_CHUNK
    wa, wd = KV_HEADS * HEAD_DIM, HEADS * HEAD_DIM
    return pl.pallas_call(
        _vt_kernel,
        out_shape=jax.ShapeDtypeStruct((_VT_HEADS, t // tk, _VT_ROWS, tk), BF16),
        grid=(t // tk,),
        in_specs=[pl.BlockSpec((tk, wa), lambda i: (i, A_V * HEAD_DIM // wa)),
                  pl.BlockSpec((tk, wd), lambda i: (i, D_V * HEAD_DIM // wd))],
        out_specs=pl.BlockSpec((_VT_HEADS, None, _VT_ROWS, tk), lambda i: (0, i, 0, 0)),
        compiler_params=_cparams(("parallel",), 8 * tk * (wa + wd) * 4),
        name="values_transposed",
    )(proj, proj)


def _pipelined(n, produce, consume, bufs, carry):
    if n == 1:
        produce(0, bufs[0])
        return consume(0, bufs[0], carry)
    assert n % 2 == 0
    produce(0, bufs[0])

    def body(c, carry):
        produce(2 * c + 1, bufs[1])
        carry = consume(2 * c, bufs[0], carry)
        produce(2 * c + 2, bufs[0])
        return consume(2 * c + 1, bufs[1], carry)

    carry = lax.fori_loop(0, n // 2 - 1, body, carry)
    produce(n - 1, bufs[1])
    carry = consume(n - 2, bufs[0], carry)
    return consume(n - 1, bufs[1], carry)


def _logits_scratch(rows, cols):
    return [pltpu.VMEM((rows, cols), F32), pltpu.VMEM((rows, cols), F32)]


def _flash_sweep(load_q_rows, n_qblocks, k_ref, vt_ref, acc_scr, q_scr, s_bufs, *, seq, rows):
    tk = KEY_CHUNK
    n = seq // tk
    acc_scr[...] = jnp.zeros(acc_scr.shape, F32)
    for qb in range(n_qblocks):
        q_scr[qb] = load_q_rows(qb)

    def logits(t, dst):
        qb, c = lax.div(t, n), lax.rem(t, n)
        k_c = k_ref[pl.ds(pl.multiple_of(c * tk, tk), tk), :]
        dst[...] = lax.dot_general(k_c, q_scr[qb], (((1,), (1,)), ((), ())),
                                   preferred_element_type=F32)

    def update(t, src, m):
        qb, c = lax.div(t, n), lax.rem(t, n)
        s = src[...]
        m = jnp.where(c == 0, -jnp.inf, m)
        m_new = jnp.maximum(m, jnp.max(s, axis=0, keepdims=True))
        alpha = jnp.exp2(m - m_new)
        p = jnp.exp2(s - m_new).astype(BF16)
        pv = jnp.dot(vt_ref[c], p, preferred_element_type=F32)
        acc_scr[qb] = alpha * acc_scr[qb] + pv
        return m_new

    _pipelined(n_qblocks * n, logits, update, s_bufs, jnp.full((1, rows), -jnp.inf, F32))


def _flash_result(acc_scr, qb):
    acc = acc_scr[qb]
    return acc[:HEAD_DIM, :] * (1.0 / acc[HEAD_DIM:HEAD_DIM + 1, :])


def _flash_qblocks(seq, tq):
    want = max(1, FLASH_ITEMS_PER_STEP // (seq // KEY_CHUNK))
    return _pick(seq // tq, tuple(c for c in (8, 4, 2, 1) if c <= want))


def _flash_scratch(n_qblocks, rows):
    return ([pltpu.VMEM((n_qblocks, _VT_ROWS, rows), F32),
             pltpu.VMEM((n_qblocks, rows, HEAD_DIM), BF16)] + _logits_scratch(KEY_CHUNK, rows))


def _flash_vmem(seq, rows, n_qblocks):
    blocks = 2 * (seq * HEAD_DIM * 2 + seq * _VT_ROWS * 2 + 2 * n_qblocks * rows * HEAD_DIM * 2)
    return blocks + n_qblocks * rows * _VT_ROWS * 4 + 12 * rows * KEY_CHUNK * 4


def _attn_a_kernel(q_ref, k_ref, vt_ref, o_ref, acc_scr, q_scr, s0_scr, s1_scr,
                   *, seq, tq, nqb):
    def load_q_rows(qb):
        q = q_ref[qb * tq:(qb + 1) * tq, :]
        return jnp.concatenate(
            [q[:, g * HEAD_DIM:(g + 1) * HEAD_DIM] for g in range(GROUP)], axis=0)

    _flash_sweep(load_q_rows, nqb, k_ref, vt_ref, acc_scr, q_scr, (s0_scr, s1_scr),
                 seq=seq, rows=GROUP * tq)
    for qb in range(nqb):
        out = _flash_result(acc_scr, qb).T
        for g in range(GROUP):
            o_ref[qb * tq:(qb + 1) * tq, g * HEAD_DIM:(g + 1) * HEAD_DIM] = (
                out[g * tq:(g + 1) * tq].astype(o_ref.dtype))


def _attn_a(qkv, vt, *, seq, nseq, tok0):
    tq = 128
    nqb = _flash_qblocks(seq, tq)
    ts = nqb * tq
    nq = seq // ts
    nc = seq // KEY_CHUNK
    qb0, sb0 = tok0 // ts, tok0 // seq
    rows = GROUP * tq
    gw = GROUP * HEAD_DIM
    return pl.pallas_call(
        functools.partial(_attn_a_kernel, seq=seq, tq=tq, nqb=nqb),
        out_shape=jax.ShapeDtypeStruct((nseq * seq, MIX_Q), BF16),
        grid=(nseq, KV_HEADS, nq),
        in_specs=[
            pl.BlockSpec((ts, gw), lambda b, h, i: (qb0 + b * nq + i, A_Q // GROUP + h)),
            pl.BlockSpec((seq, HEAD_DIM), lambda b, h, i: (sb0 + b, A_K + h)),
            pl.BlockSpec((None, nc, _VT_ROWS, KEY_CHUNK), lambda b, h, i: (h, sb0 + b, 0, 0)),
        ],
        out_specs=pl.BlockSpec((ts, gw), lambda b, h, i: (b * nq + i, h)),
        scratch_shapes=_flash_scratch(nqb, rows),
        compiler_params=_cparams(("parallel", "parallel", "arbitrary"),
                                 _flash_vmem(seq, rows, nqb)),
        name="attn_global_gqa",
    )(qkv, qkv, vt)


def _attn_d_kernel(q_ref, k_ref, vt_ref, lam_ref, subln_ref, o_ref, acc_scr, q_scr, s0_scr,
                   s1_scr, *, seq, tq, nqb, lambda_init):
    lane = lax.broadcasted_iota(jnp.int32, (tq, HEAD_DIM), 1)

    def load_q_rows(qb):
        q = q_ref[qb * tq:(qb + 1) * tq, :]
        zero = jnp.zeros_like(q)
        return jnp.concatenate([jnp.where(lane < SUB_DIM, q, zero),
                                jnp.where(lane < SUB_DIM, zero, q)], axis=0)

    _flash_sweep(load_q_rows, nqb, k_ref, vt_ref, acc_scr, q_scr, (s0_scr, s1_scr),
                 seq=seq, rows=2 * tq)
    lp = lam_ref[...]
    lam = (jnp.exp(jnp.sum(lp[0:1] * lp[1:2], axis=1, keepdims=True))
           - jnp.exp(jnp.sum(lp[2:3] * lp[3:4], axis=1, keepdims=True)) + lambda_init)
    for qb in range(nqb):
        out_t = _flash_result(acc_scr, qb)
        d = (out_t[:, :tq] - lam * out_t[:, tq:]).T
        ms = jnp.mean(d * d, axis=-1, keepdims=True)
        y = d * lax.rsqrt(ms + EPS) * subln_ref[...] * (1.0 - lambda_init)
        o_ref[qb * tq:(qb + 1) * tq, :] = y.astype(o_ref.dtype)


def _attn_d(qkv, vt, lam_params, subln, *, seq, nseq, tok0, lambda_init):
    tq = _pick(seq, (256, 128))
    nqb = _flash_qblocks(seq, tq)
    ts = nqb * tq
    nq = seq // ts
    nc = seq // KEY_CHUNK
    qb0, sb0 = tok0 // ts, tok0 // seq
    rows = 2 * tq
    return pl.pallas_call(
        functools.partial(_attn_d_kernel, seq=seq, tq=tq, nqb=nqb, lambda_init=lambda_init),
        out_shape=jax.ShapeDtypeStruct((nseq * seq, MIX_Q), BF16),
        grid=(nseq, HEADS, nq),
        in_specs=[
            pl.BlockSpec((ts, HEAD_DIM), lambda b, h, i: (qb0 + b * nq + i, D_Q + h)),
            pl.BlockSpec((seq, HEAD_DIM), lambda b, h, i: (sb0 + b, D_K + h)),
            pl.BlockSpec((None, nc, _VT_ROWS, KEY_CHUNK),
                         lambda b, h, i: (KV_HEADS + h, sb0 + b, 0, 0)),
            pl.BlockSpec((4, SUB_DIM), lambda b, h, i: (0, 0)),
            pl.BlockSpec((1, HEAD_DIM), lambda b, h, i: (0, 0)),
        ],
        out_specs=pl.BlockSpec((ts, HEAD_DIM), lambda b, h, i: (b * nq + i, h)),
        scratch_shapes=_flash_scratch(nqb, rows),
        compiler_params=_cparams(("parallel", "parallel", "arbitrary"),
                                 _flash_vmem(seq, rows, nqb)),
        name="attn_differential",
    )(qkv, qkv, vt, lam_params, subln)


_B_KEYS = 3 * WINDOW


def _window_mask_table():
    o = np.arange(3)[:, None, None] * WINDOW
    r = np.arange(WINDOW)[None, :, None]
    c = np.arange(_B_KEYS)[None, None, :]
    return jnp.asarray(np.where(np.abs(o + r - c) <= WINDOW, 0.0, MASK_VALUE), F32)


def _attn_b_kernel(sink_ref, q_ref, k_ref, v_ref, mask_ref, o_ref, s0_scr, s1_scr,
                   *, seq, blocks_per_step):
    h = pl.program_id(1)
    step = pl.program_id(2)
    tq = WINDOW
    sink = jnp.concatenate(
        [jnp.full((tq, 1), sink_ref[h * GROUP + g], F32) for g in range(GROUP)], axis=0)

    def band_start(u):
        n = step * blocks_per_step + u
        kstart = jnp.clip((n - 1) * tq, 0, seq - _B_KEYS)
        return pl.multiple_of(kstart, tq), (n * tq - kstart) // tq

    def logits(u, dst):
        kstart, _ = band_start(u)
        q = q_ref[pl.ds(pl.multiple_of(u * tq, tq), tq), :]
        q_rows = jnp.concatenate(
            [q[:, g * HEAD_DIM:(g + 1) * HEAD_DIM] for g in range(GROUP)], axis=0)
        dst[...] = lax.dot_general(q_rows, k_ref[pl.ds(kstart, _B_KEYS), :],
                                   (((1,), (1,)), ((), ())),
                                   preferred_element_type=F32)

    def attend(u, src, carry):
        kstart, offset = band_start(u)
        mask = mask_ref[offset]
        s = src[...] + jnp.concatenate([mask] * GROUP, axis=0)
        m = jnp.maximum(jnp.max(s, axis=1, keepdims=True), sink)
        p = jnp.exp(s - m)
        denom = jnp.sum(p, axis=1, keepdims=True) + jnp.exp(sink - m)
        out = jnp.dot(p.astype(BF16), v_ref[pl.ds(kstart, _B_KEYS), :],
                      preferred_element_type=F32) / denom
        rows = pl.ds(pl.multiple_of(u * tq, tq), tq)
        for g in range(GROUP):
            o_ref[rows, g * HEAD_DIM:(g + 1) * HEAD_DIM] = (
                out[g * tq:(g + 1) * tq].astype(o_ref.dtype))
        return carry

    _pipelined(blocks_per_step, logits, attend, (s0_scr, s1_scr), 0)


def _attn_b(qkv, sink, *, seq, nseq, tok0):
    nblk = _pick(seq // WINDOW, (16, 8, 4, 2))
    ts = nblk * WINDOW
    assert seq >= _B_KEYS and seq % ts == 0
    nq = seq // ts
    qb0, sb0 = tok0 // ts, tok0 // seq
    gw = GROUP * HEAD_DIM
    rows = GROUP * WINDOW
    vmem = (2 * (2 * seq * HEAD_DIM * 2 + 2 * ts * gw * 2) + 3 * WINDOW * _B_KEYS * 4
            + 10 * rows * _B_KEYS * 4)
    return pl.pallas_call(
        functools.partial(_attn_b_kernel, seq=seq, blocks_per_step=nblk),
        out_shape=jax.ShapeDtypeStruct((nseq * seq, MIX_Q), BF16),
        grid_spec=pltpu.PrefetchScalarGridSpec(
            num_scalar_prefetch=1,
            grid=(nseq, KV_HEADS, nq),
            in_specs=[
                pl.BlockSpec((ts, gw), lambda b, h, i, s_: (qb0 + b * nq + i, B_Q // GROUP + h)),
                pl.BlockSpec((seq, HEAD_DIM), lambda b, h, i, s_: (sb0 + b, B_K + h)),
                pl.BlockSpec((seq, HEAD_DIM), lambda b, h, i, s_: (sb0 + b, B_V + h)),
                pl.BlockSpec((3, WINDOW, _B_KEYS), lambda b, h, i, s_: (0, 0, 0)),
            ],
            out_specs=pl.BlockSpec((ts, gw), lambda b, h, i, s_: (b * nq + i, h)),
            scratch_shapes=_logits_scratch(rows, _B_KEYS),
        ),
        compiler_params=_cparams(("parallel", "parallel", "arbitrary"), vmem),
        name="attn_window_sink",
    )(sink, qkv, qkv, qkv, _window_mask_table())


_C_KEYS = NA_KH * GRID_W
_C_BIAS_ROWS = 2 * NA_KH - 2
_C_ROWS_PER_STAGE = 4


def _attn_c_kernel(q_ref, k_ref, v_ref, bias_ref, o_ref, s0_scr, s1_scr,
                   *, grid_rows, rows_per_step):
    i = pl.program_id(2)
    stage_rows = _C_ROWS_PER_STAGE * GRID_W

    def window(st, t):
        r = i * rows_per_step + st * _C_ROWS_PER_STAGE + t
        rs = jnp.clip(r - NA_KH // 2, 0, grid_rows - NA_KH)
        return pl.multiple_of(rs * GRID_W, GRID_W), rs - r + NA_KH - 1

    def logits(st, dst):
        for t in range(_C_ROWS_PER_STAGE):
            kstart, _ = window(st, t)
            q = q_ref[pl.ds(pl.multiple_of(st * stage_rows + t * GRID_W, GRID_W), GRID_W), :]
            dst[t * GRID_W:(t + 1) * GRID_W, :] = lax.dot_general(
                q, k_ref[pl.ds(kstart, _C_KEYS), :], (((1,), (1,)), ((), ())),
                preferred_element_type=F32)

    def attend(st, src, carry):
        wins = [window(st, t) for t in range(_C_ROWS_PER_STAGE)]
        bias = jnp.concatenate(
            [jnp.concatenate([bias_ref[dr0 + 2 * u] for u in range(NA_KH // 2)], axis=1)
             for _, dr0 in wins], axis=0)
        s = src[...] + bias
        m = jnp.max(s, axis=1, keepdims=True)
        p = jnp.exp(s - m)
        denom = jnp.sum(p, axis=1, keepdims=True)
        pb = p.astype(BF16)
        out = jnp.concatenate(
            [jnp.dot(pb[t * GRID_W:(t + 1) * GRID_W], v_ref[pl.ds(kstart, _C_KEYS), :],
                     preferred_element_type=F32) for t, (kstart, _) in enumerate(wins)], axis=0)
        rows = pl.ds(pl.multiple_of(st * stage_rows, stage_rows), stage_rows)
        o_ref[rows, :] = (out / denom).astype(o_ref.dtype)
        return carry

    _pipelined(rows_per_step // _C_ROWS_PER_STAGE, logits, attend, (s0_scr, s1_scr), 0)


def _neighbourhood_bias(rpb):
    c = np.arange(GRID_W)[:, None]
    kc = np.arange(GRID_W)[None, :]
    cs = np.clip(c - NA_KW // 2, 0, GRID_W - NA_KW)
    inside = (kc >= cs) & (kc < cs + NA_KW)
    dc = np.clip(kc - c + NA_KW - 1, 0, 2 * NA_KW - 2)
    select = np.zeros((2 * NA_KW - 1, GRID_W * GRID_W), np.float32)
    select[dc.reshape(-1), np.arange(GRID_W * GRID_W)] = 1.0
    base = jnp.einsum("lhdj,jx->lhdx", rpb.astype(F32), jnp.asarray(select),
                      precision=lax.Precision.HIGHEST)
    base = base.reshape(rpb.shape[:3] + (GRID_W, GRID_W))
    base = jnp.where(inside, base, MASK_VALUE)
    return jnp.concatenate([base[:, :, :-1], base[:, :, 1:]], axis=-1)


def _attn_c(qkv, bias, l, *, seq, nseq, tok0):
    grid_rows = seq // GRID_W
    assert grid_rows >= NA_KH
    rows_per_step = _pick(grid_rows, (32, 16, 8))
    tq = rows_per_step * GRID_W
    nq = seq // tq
    qb0, sb0 = tok0 // tq, tok0 // seq
    vmem = (2 * (2 * seq * HEAD_DIM * 2 + 2 * tq * HEAD_DIM * 2
                 + _C_BIAS_ROWS * GRID_W * 2 * GRID_W * 4)
            + 12 * _C_ROWS_PER_STAGE * GRID_W * _C_KEYS * 4)
    return pl.pallas_call(
        functools.partial(_attn_c_kernel, grid_rows=grid_rows, rows_per_step=rows_per_step),
        out_shape=jax.ShapeDtypeStruct((nseq * seq, MIX_Q), BF16),
        grid=(nseq, HEADS, nq),
        in_specs=[
            pl.BlockSpec((tq, HEAD_DIM), lambda b, h, i: (qb0 + b * nq + i, C_Q + h)),
            pl.BlockSpec((seq, HEAD_DIM), lambda b, h, i: (sb0 + b, C_K + h)),
            pl.BlockSpec((seq, HEAD_DIM), lambda b, h, i: (sb0 + b, C_V + h)),
            pl.BlockSpec((None, None, _C_BIAS_ROWS, GRID_W, 2 * GRID_W),
                         lambda b, h, i: (l, h, 0, 0, 0)),
        ],
        out_specs=pl.BlockSpec((tq, HEAD_DIM), lambda b, h, i: (b * nq + i, h)),
        scratch_shapes=_logits_scratch(_C_ROWS_PER_STAGE * GRID_W, _C_KEYS),
        compiler_params=_cparams(("parallel", "parallel", "arbitrary"), vmem),
        name="attn_neighbourhood",
    )(qkv, qkv, qkv, bias)


def _rope_tables(seqs):
    pos = np.concatenate([np.tile(np.arange(s), n) for s, n in seqs]).astype(np.int32)
    pos = jnp.asarray(pos)
    row = (pos // GRID_W).astype(F32)[:, None]
    col = (pos % GRID_W).astype(F32)[:, None]
    t = pos.astype(F32)[:, None]
    inv64 = ROPE_THETA ** (-jnp.arange(0, SUB_DIM, 2, dtype=F32) / SUB_DIM)[None, :]
    inv128 = ROPE_THETA ** (-jnp.arange(0, HEAD_DIM, 2, dtype=F32) / HEAD_DIM)[None, :]

    def pair(ang):
        c, s = jnp.cos(ang), jnp.sin(ang)
        return jnp.concatenate([c, c], axis=1), jnp.concatenate([-s, s], axis=1)

    cr, sr = pair(row * inv64)
    cc, sc = pair(col * inv64)
    cb, sb = pair(t * inv128)
    cd, sd = pair(t * inv64)
    return jnp.concatenate([cr, cc, sr, sc, cb, sb, cd, cd, sd, sd], axis=1)


def _qkv_gains(p, l):
    scale = HEAD_DIM ** -0.5
    scale_d = SUB_DIM ** -0.5
    log2e = math.log2(math.e)
    ones = lambda n: jnp.ones((n * HEAD_DIM,), F32)
    tile = lambda g, n: jnp.tile(g.astype(F32), n)
    parts = [
        tile(p["q_norm_a"][l], HEADS) * (scale * log2e), tile(p["k_norm_a"][l], KV_HEADS),
        ones(KV_HEADS),
        tile(p["q_norm_b"][l], HEADS) * scale, tile(p["k_norm_b"][l], KV_HEADS), ones(KV_HEADS),
        tile(p["q_norm_c"][l], HEADS) * scale, tile(p["k_norm_c"][l], HEADS), ones(HEADS),
        tile(p["q_norm_d"][l], 2 * HEADS) * (scale_d * log2e), tile(p["k_norm_d"][l], 2 * HEADS),
        ones(HEADS),
    ]
    return jnp.concatenate(parts).reshape(1, QKV_BLOCKS * HEAD_DIM)


def _ffn(x, g, wg, wu, wd, l):
    xn = _rmsnorm(x, g)
    hidden = _gateup(xn, wg, wu, l)
    tk_candidates = tuple(m * MXU_DIM for m in (11, 8, 4, 2, 1))
    return _mm_res(hidden, wd, l, x, FFN_RESIDUAL, tn_candidates=(1024, 512, 256, 128),
                   tk_candidates=tk_candidates)


def _token_mixing(x, p, l, seqs, rope, bias, w_in, w_out):
    lambda_init = 0.8 - 0.6 * math.exp(-0.3 * l)
    xn = _rmsnorm(x, p["norm_mix"][l])
    proj = _mm(xn, w_in, l)
    qkv = _prep(proj, _qkv_gains(p, l), rope)
    vt = _values_transposed(proj)
    sink = p["sink_b"][l].astype(F32)
    lam_params = jnp.stack([p["lambda_q1"][l], p["lambda_k1"][l],
                            p["lambda_q2"][l], p["lambda_k2"][l]]).astype(F32)
    subln = p["subln_d"][l].astype(F32).reshape(1, HEAD_DIM)
    outs = []
    tok0 = 0
    for seq, nseq in seqs:
        kw = dict(seq=seq, nseq=nseq, tok0=tok0)
        outs.append(jnp.concatenate([
            _attn_a(qkv, vt, **kw),
            _attn_b(qkv, sink, **kw),
            _attn_c(qkv, bias, l, **kw),
            _attn_d(qkv, vt, lam_params, subln, lambda_init=lambda_init, **kw),
        ], axis=1))
        tok0 += seq * nseq
    mixed = jnp.concatenate(outs, axis=0)
    return _mm_res(mixed, w_out, l, x, 1.0, tn_candidates=(512, 256, 128),
                   tk_candidates=(mixed.shape[1],))


def kernel(x_prompt, x_sample, norm_ffn1, w_ffn1_gate, w_ffn1_up, w_ffn1_down, norm_mix, w_in,
           q_norm_a, k_norm_a, q_norm_b, k_norm_b, sink_b, q_norm_c, k_norm_c, rpb_c,
           q_norm_d, k_norm_d, lambda_q1, lambda_k1, lambda_q2, lambda_k2, subln_d, w_out,
           norm_ffn2, w_ffn2_gate, w_ffn2_up, w_ffn2_down):
    p = dict(norm_mix=norm_mix, q_norm_a=q_norm_a, k_norm_a=k_norm_a, q_norm_b=q_norm_b,
             k_norm_b=k_norm_b, sink_b=sink_b, q_norm_c=q_norm_c, k_norm_c=k_norm_c, rpb_c=rpb_c,
             q_norm_d=q_norm_d, k_norm_d=k_norm_d, lambda_q1=lambda_q1, lambda_k1=lambda_k1,
             lambda_q2=lambda_q2, lambda_k2=lambda_k2, subln_d=subln_d)
    depth = w_in.shape[0]
    d_model = x_prompt.shape[-1]
    bp, sp, _ = x_prompt.shape
    bs, ss, _ = x_sample.shape
    seqs = ((sp, bp), (ss, bs))
    assert (bp * sp) % ss == 0
    x = jnp.concatenate([x_prompt.reshape(bp * sp, d_model),
                         x_sample.reshape(bs * ss, d_model)], axis=0)
    rope = _rope_tables(seqs)
    bias = _neighbourhood_bias(rpb_c)
    ffn1 = [w.astype(BF16) for w in (w_ffn1_gate, w_ffn1_up, w_ffn1_down)]
    ffn2 = [w.astype(BF16) for w in (w_ffn2_gate, w_ffn2_up, w_ffn2_down)]
    w_in_b = w_in.astype(BF16)
    w_out_b = w_out.astype(BF16)
    for l in range(depth):
        x = _ffn(x, norm_ffn1[l], *ffn1, l)
        x = _token_mixing(x, p, l, seqs, rope, bias, w_in_b, w_out_b)
        x = _ffn(x, norm_ffn2[l], *ffn2, l)
    y_prompt = x[:bp * sp].reshape(bp, sp, d_model)
    y_sample = x[bp * sp:].reshape(bs, ss, d_model)
    return (y_prompt, y_sample)
```

```python
import functools
import math

import numpy as np
import jax
import jax.numpy as jnp
from jax import lax
from jax.experimental import pallas as pl
from jax.experimental.pallas import tpu as pltpu

F32 = jnp.float32
BF16 = jnp.bfloat16

HEAD_DIM = 128
SUB_DIM = HEAD_DIM // 2
HEADS = 8
KV_HEADS = 2
GROUP = HEADS // KV_HEADS
GRID_W = 64
NA_KH = 8
NA_KW = 16
WINDOW = 128
ROPE_THETA = 10000.0
EPS = 1e-6
FFN_RESIDUAL = 0.5
MASK_VALUE = -1e30

QKV_BLOCKS = 72
A_Q, A_K, A_V = 0, 8, 10
B_Q, B_K, B_V = 12, 20, 22
C_Q, C_K, C_V = 24, 32, 40
D_Q, D_K, D_V = 48, 56, 64
MIX_Q = HEADS * HEAD_DIM
_MIX_A, _MIX_B, _MIX_C, _MIX_D = 0, MIX_Q, 2 * MIX_Q, 3 * MIX_Q

VMEM_CAP_BYTES = 60 * 1024 * 1024
MIB = 1024 * 1024
MXU_DIM = 256

FFN_TN = 256
FFN_PAD = 512
KEY_CHUNK = 512
FLASH_ITEMS_PER_STEP = 64


def _cparams(semantics, vmem_bytes):
    return pltpu.CompilerParams(
        dimension_semantics=semantics,
        vmem_limit_bytes=int(min(VMEM_CAP_BYTES, max(vmem_bytes, 16 * MIB))))


def _pick(n, candidates):
    for c in candidates:
        if n % c == 0:
            return c
    return n


def _rmsnorm_kernel(x_ref, g_ref, o_ref):
    x = x_ref[...]
    ms = jnp.mean(x * x, axis=-1, keepdims=True)
    o_ref[...] = (x * lax.rsqrt(ms + EPS) * g_ref[...]).astype(o_ref.dtype)


def _rmsnorm(x, g):
    t, d = x.shape
    tm = _pick(t, (256, 128, 64, 32, 16, 8))
    return pl.pallas_call(
        _rmsnorm_kernel,
        out_shape=jax.ShapeDtypeStruct((t, d), BF16),
        grid=(t // tm,),
        in_specs=[pl.BlockSpec((tm, d), lambda i: (i, 0)),
                  pl.BlockSpec((1, d), lambda i: (0, 0))],
        out_specs=pl.BlockSpec((tm, d), lambda i: (i, 0)),
        compiler_params=_cparams(("parallel",), 2 * tm * d * 6 + 4 * tm * d * 4),
        name="rmsnorm",
    )(x, g.reshape(1, d).astype(F32))


def _gateup_kernel(x_ref, wg_ref, wu_ref, o_ref, *, d_ff, padded):
    x = x_ref[...]
    g = jnp.dot(x, wg_ref[...].astype(BF16), preferred_element_type=F32)
    u = jnp.dot(x, wu_ref[...].astype(BF16), preferred_element_type=F32)
    h = g * (1.0 / (1.0 + jnp.exp(-g))) * u
    if padded:
        tn = o_ref.shape[1]
        col = pl.program_id(1) * tn + lax.broadcasted_iota(jnp.int32, h.shape, 1)
        h = jnp.where(col < d_ff, h, 0.0)
    o_ref[...] = h.astype(o_ref.dtype)


def _gateup(xn, wg, wu, l):
    t, d = xn.shape
    d_ff = wg.shape[2]
    tm = _pick(t, (1024, 512, 256, 128))
    tn = FFN_TN
    width = -(-d_ff // FFN_PAD) * FFN_PAD
    last = pl.cdiv(d_ff, tn) - 1
    wbytes = wg.dtype.itemsize
    vmem = (2 * (tm * d * 2 + 2 * d * tn * wbytes + tm * tn * 2) + 2 * d * tn * 2
            + 8 * tm * tn * 4)
    w_spec = pl.BlockSpec((None, d, tn), lambda i, j: (l, 0, jnp.minimum(j, last)))
    return pl.pallas_call(
        functools.partial(_gateup_kernel, d_ff=d_ff, padded=width != d_ff),
        out_shape=jax.ShapeDtypeStruct((t, width), BF16),
        grid=(t // tm, width // tn),
        in_specs=[pl.BlockSpec((tm, d), lambda i, j: (i, 0)), w_spec, w_spec],
        out_specs=pl.BlockSpec((tm, tn), lambda i, j: (i, j)),
        compiler_params=_cparams(("parallel", "arbitrary"), vmem),
        name="ffn_gateup",
    )(xn, wg, wu)


def _mm_res_kernel(a_ref, w_ref, r_ref, o_ref, *scratch, nk, scale, k_valid):
    tk = w_ref.shape[0]

    def partial_sum(mask_tail):
        w = w_ref[...].astype(BF16)
        if mask_tail:
            row = (nk - 1) * tk + lax.broadcasted_iota(jnp.int32, w.shape, 0)
            w = jnp.where(row < k_valid, w, jnp.zeros_like(w))
        return jnp.dot(a_ref[...], w, preferred_element_type=F32)

    ragged = k_valid < nk * tk
    if nk == 1:
        o_ref[...] = r_ref[...] + scale * partial_sum(ragged)
        return
    acc_ref, = scratch
    k = pl.program_id(2)

    @pl.when(k == 0)
    def _():
        acc_ref[...] = partial_sum(False)

    @pl.when(jnp.logical_and(k > 0, k < nk - 1))
    def _():
        acc_ref[...] += partial_sum(False)

    @pl.when(k == nk - 1)
    def _():
        o_ref[...] = r_ref[...] + scale * (acc_ref[...] + partial_sum(ragged))


def _mm_res(a, w, l, res, scale, *, tn_candidates, tk_candidates):
    t, ka = a.shape
    k_valid, n = w.shape[1], w.shape[2]
    tm = _pick(t, (1024, 512, 256, 128))
    tn = _pick(n, tn_candidates)
    tk = _pick(ka, tk_candidates)
    nk = ka // tk
    assert (nk - 1) * tk < k_valid <= ka
    scratch = [pltpu.VMEM((tm, tn), F32)] if nk > 1 else []
    vmem = (2 * (tm * tk * 2 + tk * tn * w.dtype.itemsize + 2 * tm * tn * 4) + tk * tn * 2
            + 3 * tm * tn * 4)
    return pl.pallas_call(
        functools.partial(_mm_res_kernel, nk=nk, scale=scale, k_valid=k_valid),
        out_shape=jax.ShapeDtypeStruct((t, n), F32),
        grid=(t // tm, n // tn, nk),
        in_specs=[pl.BlockSpec((tm, tk), lambda i, j, k: (i, k)),
                  pl.BlockSpec((None, tk, tn), lambda i, j, k: (l, k, j)),
                  pl.BlockSpec((tm, tn), lambda i, j, k: (i, j))],
        out_specs=pl.BlockSpec((tm, tn), lambda i, j, k: (i, j)),
        scratch_shapes=scratch,
        compiler_params=_cparams(("parallel", "parallel", "arbitrary"), vmem),
        name="matmul_residual",
    )(a, w, res)


def _mm_kernel(a_ref, w_ref, o_ref):
    o_ref[...] = jnp.dot(a_ref[...], w_ref[...].astype(BF16), preferred_element_type=F32)


def _mm(a, w, l):
    t, kdim = a.shape
    n = w.shape[2]
    tm = _pick(t, (1024, 512, 256, 128))
    tn = _pick(n, (512, 256, 128))
    vmem = (2 * (tm * kdim * 2 + kdim * tn * w.dtype.itemsize + tm * tn * 4) + kdim * tn * 2
            + 2 * tm * tn * 4)
    return pl.pallas_call(
        _mm_kernel,
        out_shape=jax.ShapeDtypeStruct((t, n), F32),
        grid=(t // tm, n // tn),
        in_specs=[pl.BlockSpec((tm, kdim), lambda i, j: (i, 0)),
                  pl.BlockSpec((None, kdim, tn), lambda i, j: (l, 0, j))],
        out_specs=pl.BlockSpec((tm, tn), lambda i, j: (i, j)),
        compiler_params=_cparams(("parallel", "arbitrary"), vmem),
        name="in_proj",
    )(a, w)


_PREP_CHUNK_KINDS = (
    "a" * (HEADS + KV_HEADS) + "v" * KV_HEADS + "b" * (HEADS + KV_HEADS) + "v" * KV_HEADS
    + "c" * (2 * HEADS) + "v" * HEADS + "d" * (2 * HEADS) + "v" * HEADS)
_PREP_CHUNKS = 8
_PREP_COLS = _PREP_CHUNKS * HEAD_DIM
_PREP_BLOCK_KINDS = tuple(
    tuple(_PREP_CHUNK_KINDS[i:i + _PREP_CHUNKS]) for i in range(0, QKV_BLOCKS, _PREP_CHUNKS))
_ROPE_OFF = {"a": 0, "b": 2 * HEAD_DIM, "d": 4 * HEAD_DIM}


def _prep_kernel(p_ref, g_ref, rope_ref, o_ref):
    j = pl.program_id(1)
    tm = p_ref.shape[0]
    lane = lax.broadcasted_iota(jnp.int32, (tm, HEAD_DIM), 1)
    low_half = lane < SUB_DIM
    low_quarter = (lane & (SUB_DIM - 1)) < SUB_DIM // 2

    def chunk(c, kind):
        cols = slice(c * HEAD_DIM, (c + 1) * HEAD_DIM)
        x = p_ref[:, cols]
        if kind == "v":
            o_ref[:, cols] = x.astype(o_ref.dtype)
            return
        sq = x * x
        if kind == "d":
            lo = jnp.sum(jnp.where(low_half, sq, 0.0), axis=-1, keepdims=True)
            hi = jnp.sum(jnp.where(low_half, 0.0, sq), axis=-1, keepdims=True)
            ms = jnp.where(low_half, lo, hi) * (1.0 / SUB_DIM)
        else:
            ms = jnp.mean(sq, axis=-1, keepdims=True)
        y = x * lax.rsqrt(ms + EPS) * g_ref[:, cols]
        if kind != "c":
            off = _ROPE_OFF[kind]
            cos = rope_ref[:, off:off + HEAD_DIM]
            sin = rope_ref[:, off + HEAD_DIM:off + 2 * HEAD_DIM]
            if kind == "b":
                partner = pltpu.roll(y, SUB_DIM, 1)
            else:
                partner = jnp.where(low_quarter,
                                    pltpu.roll(y, HEAD_DIM - SUB_DIM // 2, 1),
                                    pltpu.roll(y, SUB_DIM // 2, 1))
            y = y * cos + partner * sin
        o_ref[:, cols] = y.astype(o_ref.dtype)

    groups = {}
    for jj, kinds in enumerate(_PREP_BLOCK_KINDS):
        groups.setdefault(kinds, []).append(jj)
    for kinds, js in groups.items():
        cond = functools.reduce(jnp.logical_or, [j == jj for jj in js])

        @pl.when(cond)
        def _(kinds=kinds):
            for c, kind in enumerate(kinds):
                chunk(c, kind)


def _prep(proj, gains, rope):
    t, w = proj.shape
    assert w == QKV_BLOCKS * HEAD_DIM
    tm = _pick(t, (512, 256, 128, 64, 32, 16))
    rw = rope.shape[1]
    vmem = 2 * (tm * _PREP_COLS * 6 + tm * rw * 4) + 8 * tm * _PREP_COLS * 4
    return pl.pallas_call(
        _prep_kernel,
        out_shape=jax.ShapeDtypeStruct((t, w), BF16),
        grid=(t // tm, w // _PREP_COLS),
        in_specs=[pl.BlockSpec((tm, _PREP_COLS), lambda i, j: (i, j)),
                  pl.BlockSpec((1, _PREP_COLS), lambda i, j: (0, j)),
                  pl.BlockSpec((tm, rw), lambda i, j: (i, 0))],
        out_specs=pl.BlockSpec((tm, _PREP_COLS), lambda i, j: (i, j)),
        compiler_params=_cparams(("parallel", "arbitrary"), vmem),
        name="qkv_norm_rope",
    )(proj, gains, rope)


_VT_HEADS = KV_HEADS + HEADS
_BF16_SUBLANES = 16
_VT_ROWS = HEAD_DIM + _BF16_SUBLANES


def _vt_kernel(va_ref, vd_ref, o_ref):
    ones = jnp.ones((_BF16_SUBLANES, o_ref.shape[-1]), o_ref.dtype)
    heads = ([(va_ref, h) for h in range(KV_HEADS)] + [(vd_ref, h) for h in range(HEADS)])
    for slot, (ref, h) in enumerate(heads):
        o_ref[slot, :HEAD_DIM, :] = ref[:, h * HEAD_DIM:(h + 1) * HEAD_DIM].T.astype(o_ref.dtype)
        o_ref[slot, HEAD_DIM:, :] = ones


def _values_transposed(proj):
    t = proj.shape[0]
    tk = KEY_CHUNK
    wa, wd = KV_HEADS * HEAD_DIM, HEADS * HEAD_DIM
    return pl.pallas_call(
        _vt_kernel,
        out_shape=jax.ShapeDtypeStruct((_VT_HEADS, t // tk, _VT_ROWS, tk), BF16),
        grid=(t // tk,),
        in_specs=[pl.BlockSpec((tk, wa), lambda i: (i, A_V * HEAD_DIM // wa)),
                  pl.BlockSpec((tk, wd), lambda i: (i, D_V * HEAD_DIM // wd))],
        out_specs=pl.BlockSpec((_VT_HEADS, None, _VT_ROWS, tk), lambda i: (0, i, 0, 0)),
        compiler_params=_cparams(("parallel",), 8 * tk * (wa + wd) * 4),
        name="values_transposed",
    )(proj, proj)


def _pipelined(n, produce, consume, bufs, carry):
    if n == 1:
        produce(0, bufs[0])
        return consume(0, bufs[0], carry)
    assert n % 2 == 0
    produce(0, bufs[0])

    def body(c, carry):
        produce(2 * c + 1, bufs[1])
        carry = consume(2 * c, bufs[0], carry)
        produce(2 * c + 2, bufs[0])
        return consume(2 * c + 1, bufs[1], carry)

    carry = lax.fori_loop(0, n // 2 - 1, body, carry)
    produce(n - 1, bufs[1])
    carry = consume(n - 2, bufs[0], carry)
    return consume(n - 1, bufs[1], carry)


def _logits_scratch(rows, cols):
    return [pltpu.VMEM((rows, cols), F32), pltpu.VMEM((rows, cols), F32)]


def _flash_sweep(load_q_rows, n_qblocks, k_ref, vt_ref, acc_scr, q_scr, s_bufs, *, seq, rows):
    tk = KEY_CHUNK
    n = seq // tk
    acc_scr[...] = jnp.zeros(acc_scr.shape, F32)
    for qb in range(n_qblocks):
        q_scr[qb] = load_q_rows(qb).T

    def logits(t, dst):
        qb, c = lax.div(t, n), lax.rem(t, n)
        k_c = k_ref[pl.ds(pl.multiple_of(c * tk, tk), tk), :]
        dst[...] = jnp.dot(k_c, q_scr[qb], preferred_element_type=F32)

    def update(t, src, m):
        qb, c = lax.div(t, n), lax.rem(t, n)
        s = src[...]
        m = jnp.where(c == 0, -jnp.inf, m)
        m_new = jnp.maximum(m, jnp.max(s, axis=0, keepdims=True))
        alpha = jnp.exp2(m - m_new)
        p = jnp.exp2(s - m_new).astype(BF16)
        pv = jnp.dot(vt_ref[c], p, preferred_element_type=F32)
        acc_scr[qb] = alpha * acc_scr[qb] + pv
        return m_new

    _pipelined(n_qblocks * n, logits, update, s_bufs, jnp.full((1, rows), -jnp.inf, F32))


def _flash_result(acc_scr, qb):
    acc = acc_scr[qb]
    return acc[:HEAD_DIM, :] * (1.0 / acc[HEAD_DIM:HEAD_DIM + 1, :])


def _flash_qblocks(seq, tq):
    want = max(1, FLASH_ITEMS_PER_STEP // (seq // KEY_CHUNK))
    return _pick(seq // tq, tuple(c for c in (8, 4, 2, 1) if c <= want))


def _flash_scratch(n_qblocks, rows):
    return ([pltpu.VMEM((n_qblocks, _VT_ROWS, rows), F32),
             pltpu.VMEM((n_qblocks, HEAD_DIM, rows), BF16)] + _logits_scratch(KEY_CHUNK, rows))


def _flash_vmem(seq, rows, n_qblocks):
    blocks = 2 * (seq * HEAD_DIM * 2 + seq * _VT_ROWS * 2 + 2 * n_qblocks * rows * HEAD_DIM * 2)
    return blocks + n_qblocks * rows * _VT_ROWS * 4 + 12 * rows * KEY_CHUNK * 4


def _attn_a_kernel(q_ref, k_ref, vt_ref, mixed_ref, o_ref, acc_scr, q_scr, s0_scr, s1_scr,
                   *, seq, tq, nqb):
    def load_q_rows(qb):
        q = q_ref[qb * tq:(qb + 1) * tq, :]
        return jnp.concatenate(
            [q[:, g * HEAD_DIM:(g + 1) * HEAD_DIM] for g in range(GROUP)], axis=0)

    _flash_sweep(load_q_rows, nqb, k_ref, vt_ref, acc_scr, q_scr, (s0_scr, s1_scr),
                 seq=seq, rows=GROUP * tq)
    for qb in range(nqb):
        out = _flash_result(acc_scr, qb).T
        for g in range(GROUP):
            o_ref[qb * tq:(qb + 1) * tq, g * HEAD_DIM:(g + 1) * HEAD_DIM] = (
                out[g * tq:(g + 1) * tq].astype(o_ref.dtype))


def _attn_a(qkv, vt, mixed, *, seq, nseq, tok0):
    tq = 128
    nqb = _flash_qblocks(seq, tq)
    ts = nqb * tq
    nq = seq // ts
    nc = seq // KEY_CHUNK
    qb0, sb0 = tok0 // ts, tok0 // seq
    rows = GROUP * tq
    gw = GROUP * HEAD_DIM
    return pl.pallas_call(
        functools.partial(_attn_a_kernel, seq=seq, tq=tq, nqb=nqb),
        out_shape=jax.ShapeDtypeStruct(mixed.shape, mixed.dtype),
        grid=(nseq, KV_HEADS, nq),
        in_specs=[
            pl.BlockSpec((ts, gw), lambda b, h, i: (qb0 + b * nq + i, A_Q // GROUP + h)),
            pl.BlockSpec((seq, HEAD_DIM), lambda b, h, i: (sb0 + b, A_K + h)),
            pl.BlockSpec((None, nc, _VT_ROWS, KEY_CHUNK), lambda b, h, i: (h, sb0 + b, 0, 0)),
            pl.BlockSpec(memory_space=pl.ANY),
        ],
        out_specs=pl.BlockSpec((ts, gw), lambda b, h, i: (qb0 + b * nq + i, _MIX_A // gw + h)),
        input_output_aliases={3: 0},
        scratch_shapes=_flash_scratch(nqb, rows),
        compiler_params=_cparams(("parallel", "parallel", "arbitrary"),
                                 _flash_vmem(seq, rows, nqb)),
        name="attn_global_gqa",
    )(qkv, qkv, vt, mixed)


def _attn_d_kernel(q_ref, k_ref, vt_ref, lam_ref, subln_ref, mixed_ref, o_ref, acc_scr, q_scr,
                   s0_scr, s1_scr, *, seq, tq, nqb, lambda_init):
    lane = lax.broadcasted_iota(jnp.int32, (tq, HEAD_DIM), 1)

    def load_q_rows(qb):
        q = q_ref[qb * tq:(qb + 1) * tq, :]
        zero = jnp.zeros_like(q)
        return jnp.concatenate([jnp.where(lane < SUB_DIM, q, zero),
                                jnp.where(lane < SUB_DIM, zero, q)], axis=0)

    _flash_sweep(load_q_rows, nqb, k_ref, vt_ref, acc_scr, q_scr, (s0_scr, s1_scr),
                 seq=seq, rows=2 * tq)
    lp = lam_ref[...]
    lam = (jnp.exp(jnp.sum(lp[0:1] * lp[1:2], axis=1, keepdims=True))
           - jnp.exp(jnp.sum(lp[2:3] * lp[3:4], axis=1, keepdims=True)) + lambda_init)
    for qb in range(nqb):
        out_t = _flash_result(acc_scr, qb)
        d = (out_t[:, :tq] - lam * out_t[:, tq:]).T
        ms = jnp.mean(d * d, axis=-1, keepdims=True)
        y = d * lax.rsqrt(ms + EPS) * subln_ref[...] * (1.0 - lambda_init)
        o_ref[qb * tq:(qb + 1) * tq, :] = y.astype(o_ref.dtype)


def _attn_d(qkv, vt, lam_params, subln, mixed, *, seq, nseq, tok0, lambda_init):
    tq = _pick(seq, (256, 128))
    nqb = _flash_qblocks(seq, tq)
    ts = nqb * tq
    nq = seq // ts
    nc = seq // KEY_CHUNK
    qb0, sb0 = tok0 // ts, tok0 // seq
    rows = 2 * tq
    return pl.pallas_call(
        functools.partial(_attn_d_kernel, seq=seq, tq=tq, nqb=nqb, lambda_init=lambda_init),
        out_shape=jax.ShapeDtypeStruct(mixed.shape, mixed.dtype),
        grid=(nseq, HEADS, nq),
        in_specs=[
            pl.BlockSpec((ts, HEAD_DIM), lambda b, h, i: (qb0 + b * nq + i, D_Q + h)),
            pl.BlockSpec((seq, HEAD_DIM), lambda b, h, i: (sb0 + b, D_K + h)),
            pl.BlockSpec((None, nc, _VT_ROWS, KEY_CHUNK),
                         lambda b, h, i: (KV_HEADS + h, sb0 + b, 0, 0)),
            pl.BlockSpec((4, SUB_DIM), lambda b, h, i: (0, 0)),
            pl.BlockSpec((1, HEAD_DIM), lambda b, h, i: (0, 0)),
            pl.BlockSpec(memory_space=pl.ANY),
        ],
        out_specs=pl.BlockSpec((ts, HEAD_DIM),
                               lambda b, h, i: (qb0 + b * nq + i, _MIX_D // HEAD_DIM + h)),
        input_output_aliases={5: 0},
        scratch_shapes=_flash_scratch(nqb, rows),
        compiler_params=_cparams(("parallel", "parallel", "arbitrary"),
                                 _flash_vmem(seq, rows, nqb)),
        name="attn_differential",
    )(qkv, qkv, vt, lam_params, subln, mixed)


_B_KEYS = 3 * WINDOW


def _window_mask_table():
    o = np.arange(3)[:, None, None] * WINDOW
    r = np.arange(WINDOW)[None, :, None]
    c = np.arange(_B_KEYS)[None, None, :]
    return jnp.asarray(np.where(np.abs(o + r - c) <= WINDOW, 0.0, MASK_VALUE), F32)


def _attn_b_kernel(sink_ref, q_ref, k_ref, v_ref, mask_ref, mixed_ref, o_ref, s0_scr, s1_scr,
                   *, seq, blocks_per_step):
    h = pl.program_id(1)
    step = pl.program_id(2)
    tq = WINDOW
    sink = jnp.concatenate(
        [jnp.full((tq, 1), sink_ref[h * GROUP + g], F32) for g in range(GROUP)], axis=0)

    def band_start(u):
        n = step * blocks_per_step + u
        kstart = jnp.clip((n - 1) * tq, 0, seq - _B_KEYS)
        return pl.multiple_of(kstart, tq), (n * tq - kstart) // tq

    def logits(u, dst):
        kstart, _ = band_start(u)
        q = q_ref[pl.ds(pl.multiple_of(u * tq, tq), tq), :]
        q_rows = jnp.concatenate(
            [q[:, g * HEAD_DIM:(g + 1) * HEAD_DIM] for g in range(GROUP)], axis=0)
        dst[...] = lax.dot_general(q_rows, k_ref[pl.ds(kstart, _B_KEYS), :],
                                   (((1,), (1,)), ((), ())),
                                   preferred_element_type=F32)

    def attend(u, src, carry):
        kstart, offset = band_start(u)
        mask = mask_ref[offset]
        s = src[...] + jnp.concatenate([mask] * GROUP, axis=0)
        m = jnp.maximum(jnp.max(s, axis=1, keepdims=True), sink)
        p = jnp.exp(s - m)
        denom = jnp.sum(p, axis=1, keepdims=True) + jnp.exp(sink - m)
        out = jnp.dot(p.astype(BF16), v_ref[pl.ds(kstart, _B_KEYS), :],
                      preferred_element_type=F32) / denom
        rows = pl.ds(pl.multiple_of(u * tq, tq), tq)
        for g in range(GROUP):
            o_ref[rows, g * HEAD_DIM:(g + 1) * HEAD_DIM] = (
                out[g * tq:(g + 1) * tq].astype(o_ref.dtype))
        return carry

    _pipelined(blocks_per_step, logits, attend, (s0_scr, s1_scr), 0)


def _attn_b(qkv, sink, mixed, *, seq, nseq, tok0):
    nblk = _pick(seq // WINDOW, (16, 8, 4, 2))
    ts = nblk * WINDOW
    assert seq >= _B_KEYS and seq % ts == 0
    nq = seq // ts
    qb0, sb0 = tok0 // ts, tok0 // seq
    gw = GROUP * HEAD_DIM
    rows = GROUP * WINDOW
    vmem = (2 * (2 * seq * HEAD_DIM * 2 + 2 * ts * gw * 2) + 3 * WINDOW * _B_KEYS * 4
            + 10 * rows * _B_KEYS * 4)
    return pl.pallas_call(
        functools.partial(_attn_b_kernel, seq=seq, blocks_per_step=nblk),
        out_shape=jax.ShapeDtypeStruct(mixed.shape, mixed.dtype),
        grid_spec=pltpu.PrefetchScalarGridSpec(
            num_scalar_prefetch=1,
            grid=(nseq, KV_HEADS, nq),
            in_specs=[
                pl.BlockSpec((ts, gw), lambda b, h, i, s_: (qb0 + b * nq + i, B_Q // GROUP + h)),
                pl.BlockSpec((seq, HEAD_DIM), lambda b, h, i, s_: (sb0 + b, B_K + h)),
                pl.BlockSpec((seq, HEAD_DIM), lambda b, h, i, s_: (sb0 + b, B_V + h)),
                pl.BlockSpec((3, WINDOW, _B_KEYS), lambda b, h, i, s_: (0, 0, 0)),
                pl.BlockSpec(memory_space=pl.ANY),
            ],
            out_specs=pl.BlockSpec(
                (ts, gw), lambda b, h, i, s_: (qb0 + b * nq + i, _MIX_B // gw + h)),
            scratch_shapes=_logits_scratch(rows, _B_KEYS),
        ),
        input_output_aliases={5: 0},
        compiler_params=_cparams(("parallel", "parallel", "arbitrary"), vmem),
        name="attn_window_sink",
    )(sink, qkv, qkv, qkv, _window_mask_table(), mixed)


_C_KEYS = NA_KH * GRID_W
_C_BIAS_ROWS = 2 * NA_KH - 2
_C_ROWS_PER_STAGE = 4


def _attn_c_kernel(q_ref, k_ref, v_ref, bias_ref, mixed_ref, o_ref, s0_scr, s1_scr,
                   *, grid_rows, rows_per_step):
    i = pl.program_id(2)
    stage_rows = _C_ROWS_PER_STAGE * GRID_W

    def window(st, t):
        r = i * rows_per_step + st * _C_ROWS_PER_STAGE + t
        rs = jnp.clip(r - NA_KH // 2, 0, grid_rows - NA_KH)
        return pl.multiple_of(rs * GRID_W, GRID_W), rs - r + NA_KH - 1

    def logits(st, dst):
        for t in range(_C_ROWS_PER_STAGE):
            kstart, _ = window(st, t)
            q = q_ref[pl.ds(pl.multiple_of(st * stage_rows + t * GRID_W, GRID_W), GRID_W), :]
            dst[t * GRID_W:(t + 1) * GRID_W, :] = lax.dot_general(
                q, k_ref[pl.ds(kstart, _C_KEYS), :], (((1,), (1,)), ((), ())),
                preferred_element_type=F32)

    def attend(st, src, carry):
        wins = [window(st, t) for t in range(_C_ROWS_PER_STAGE)]
        bias = jnp.concatenate(
            [jnp.concatenate([bias_ref[dr0 + 2 * u] for u in range(NA_KH // 2)], axis=1)
             for _, dr0 in wins], axis=0)
        s = src[...] + bias
        m = jnp.max(s, axis=1, keepdims=True)
        p = jnp.exp(s - m)
        denom = jnp.sum(p, axis=1, keepdims=True)
        pb = p.astype(BF16)
        out = jnp.concatenate(
            [jnp.dot(pb[t * GRID_W:(t + 1) * GRID_W], v_ref[pl.ds(kstart, _C_KEYS), :],
                     preferred_element_type=F32) for t, (kstart, _) in enumerate(wins)], axis=0)
        rows = pl.ds(pl.multiple_of(st * stage_rows, stage_rows), stage_rows)
        o_ref[rows, :] = (out / denom).astype(o_ref.dtype)
        return carry

    _pipelined(rows_per_step // _C_ROWS_PER_STAGE, logits, attend, (s0_scr, s1_scr), 0)


def _neighbourhood_bias(rpb):
    c = np.arange(GRID_W)[:, None]
    kc = np.arange(GRID_W)[None, :]
    cs = np.clip(c - NA_KW // 2, 0, GRID_W - NA_KW)
    inside = (kc >= cs) & (kc < cs + NA_KW)
    dc = np.clip(kc - c + NA_KW - 1, 0, 2 * NA_KW - 2)
    select = np.zeros((2 * NA_KW - 1, GRID_W * GRID_W), np.float32)
    select[dc.reshape(-1), np.arange(GRID_W * GRID_W)] = 1.0
    base = jnp.einsum("lhdj,jx->lhdx", rpb.astype(F32), jnp.asarray(select),
                      precision=lax.Precision.HIGHEST)
    base = base.reshape(rpb.shape[:3] + (GRID_W, GRID_W))
    base = jnp.where(inside, base, MASK_VALUE)
    return jnp.concatenate([base[:, :, :-1], base[:, :, 1:]], axis=-1)


def _attn_c(qkv, bias, l, mixed, *, seq, nseq, tok0):
    grid_rows = seq // GRID_W
    assert grid_rows >= NA_KH
    rows_per_step = _pick(grid_rows, (32, 16, 8))
    tq = rows_per_step * GRID_W
    nq = seq // tq
    qb0, sb0 = tok0 // tq, tok0 // seq
    vmem = (2 * (2 * seq * HEAD_DIM * 2 + 2 * tq * HEAD_DIM * 2
                 + _C_BIAS_ROWS * GRID_W * 2 * GRID_W * 4)
            + 12 * _C_ROWS_PER_STAGE * GRID_W * _C_KEYS * 4)
    return pl.pallas_call(
        functools.partial(_attn_c_kernel, grid_rows=grid_rows, rows_per_step=rows_per_step),
        out_shape=jax.ShapeDtypeStruct(mixed.shape, mixed.dtype),
        grid=(nseq, HEADS, nq),
        in_specs=[
            pl.BlockSpec((tq, HEAD_DIM), lambda b, h, i: (qb0 + b * nq + i, C_Q + h)),
            pl.BlockSpec((seq, HEAD_DIM), lambda b, h, i: (sb0 + b, C_K + h)),
            pl.BlockSpec((seq, HEAD_DIM), lambda b, h, i: (sb0 + b, C_V + h)),
            pl.BlockSpec((None, None, _C_BIAS_ROWS, GRID_W, 2 * GRID_W),
                         lambda b, h, i: (l, h, 0, 0, 0)),
            pl.BlockSpec(memory_space=pl.ANY),
        ],
        out_specs=pl.BlockSpec((tq, HEAD_DIM),
                               lambda b, h, i: (qb0 + b * nq + i, _MIX_C // HEAD_DIM + h)),
        input_output_aliases={4: 0},
        scratch_shapes=_logits_scratch(_C_ROWS_PER_STAGE * GRID_W, _C_KEYS),
        compiler_params=_cparams(("parallel", "parallel", "arbitrary"), vmem),
        name="attn_neighbourhood",
    )(qkv, qkv, qkv, bias, mixed)


def _rope_tables(seqs):
    pos = np.concatenate([np.tile(np.arange(s), n) for s, n in seqs]).astype(np.int32)
    pos = jnp.asarray(pos)
    row = (pos // GRID_W).astype(F32)[:, None]
    col = (pos % GRID_W).astype(F32)[:, None]
    t = pos.astype(F32)[:, None]
    inv64 = ROPE_THETA ** (-jnp.arange(0, SUB_DIM, 2, dtype=F32) / SUB_DIM)[None, :]
    inv128 = ROPE_THETA ** (-jnp.arange(0, HEAD_DIM, 2, dtype=F32) / HEAD_DIM)[None, :]

    def pair(ang):
        c, s = jnp.cos(ang), jnp.sin(ang)
        return jnp.concatenate([c, c], axis=1), jnp.concatenate([-s, s], axis=1)

    cr, sr = pair(row * inv64)
    cc, sc = pair(col * inv64)
    cb, sb = pair(t * inv128)
    cd, sd = pair(t * inv64)
    return jnp.concatenate([cr, cc, sr, sc, cb, sb, cd, cd, sd, sd], axis=1)


def _qkv_gains(p, l):
    scale = HEAD_DIM ** -0.5
    scale_d = SUB_DIM ** -0.5
    log2e = math.log2(math.e)
    ones = lambda n: jnp.ones((n * HEAD_DIM,), F32)
    tile = lambda g, n: jnp.tile(g.astype(F32), n)
    parts = [
        tile(p["q_norm_a"][l], HEADS) * (scale * log2e), tile(p["k_norm_a"][l], KV_HEADS),
        ones(KV_HEADS),
        tile(p["q_norm_b"][l], HEADS) * scale, tile(p["k_norm_b"][l], KV_HEADS), ones(KV_HEADS),
        tile(p["q_norm_c"][l], HEADS) * scale, tile(p["k_norm_c"][l], HEADS), ones(HEADS),
        tile(p["q_norm_d"][l], 2 * HEADS) * (scale_d * log2e), tile(p["k_norm_d"][l], 2 * HEADS),
        ones(HEADS),
    ]
    return jnp.concatenate(parts).reshape(1, QKV_BLOCKS * HEAD_DIM)


def _ffn(x, g, wg, wu, wd, l):
    xn = _rmsnorm(x, g)
    hidden = _gateup(xn, wg, wu, l)
    tk_candidates = tuple(m * MXU_DIM for m in (11, 8, 4, 2, 1))
    return _mm_res(hidden, wd, l, x, FFN_RESIDUAL, tn_candidates=(1024, 512, 256, 128),
                   tk_candidates=tk_candidates)


def _token_mixing(x, p, l, seqs, rope, bias, w_in, w_out):
    lambda_init = 0.8 - 0.6 * math.exp(-0.3 * l)
    xn = _rmsnorm(x, p["norm_mix"][l])
    proj = _mm(xn, w_in, l)
    qkv = _prep(proj, _qkv_gains(p, l), rope)
    vt = _values_transposed(proj)
    sink = p["sink_b"][l].astype(F32)
    lam_params = jnp.stack([p["lambda_q1"][l], p["lambda_k1"][l],
                            p["lambda_q2"][l], p["lambda_k2"][l]]).astype(F32)
    subln = p["subln_d"][l].astype(F32).reshape(1, HEAD_DIM)
    mixed = jnp.zeros((x.shape[0], 4 * MIX_Q), BF16)
    tok0 = 0
    for seq, nseq in seqs:
        kw = dict(seq=seq, nseq=nseq, tok0=tok0)
        mixed = _attn_a(qkv, vt, mixed, **kw)
        mixed = _attn_b(qkv, sink, mixed, **kw)
        mixed = _attn_c(qkv, bias, l, mixed, **kw)
        mixed = _attn_d(qkv, vt, lam_params, subln, mixed, lambda_init=lambda_init, **kw)
        tok0 += seq * nseq
    return _mm_res(mixed, w_out, l, x, 1.0, tn_candidates=(512, 256, 128),
                   tk_candidates=(mixed.shape[1],))


def kernel(x_prompt, x_sample, norm_ffn1, w_ffn1_gate, w_ffn1_up, w_ffn1_down, norm_mix, w_in,
           q_norm_a, k_norm_a, q_norm_b, k_norm_b, sink_b, q_norm_c, k_norm_c, rpb_c,
           q_norm_d, k_norm_d, lambda_q1, lambda_k1, lambda_q2, lambda_k2, subln_d, w_out,
           norm_ffn2, w_ffn2_gate, w_ffn2_up, w_ffn2_down):
    p = dict(norm_mix=norm_mix, q_norm_a=q_norm_a, k_norm_a=k_norm_a, q_norm_b=q_norm_b,
             k_norm_b=k_norm_b, sink_b=sink_b, q_norm_c=q_norm_c, k_norm_c=k_norm_c, rpb_c=rpb_c,
             q_norm_d=q_norm_d, k_norm_d=k_norm_d, lambda_q1=lambda_q1, lambda_k1=lambda_k1,
             lambda_q2=lambda_q2, lambda_k2=lambda_k2, subln_d=subln_d)
    depth = w_in.shape[0]
    d_model = x_prompt.shape[-1]
    bp, sp, _ = x_prompt.shape
    bs, ss, _ = x_sample.shape
    seqs = ((sp, bp), (ss, bs))
    assert (bp * sp) % ss == 0
    x = jnp.concatenate([x_prompt.reshape(bp * sp, d_model),
                         x_sample.reshape(bs * ss, d_model)], axis=0)
    rope = _rope_tables(seqs)
    bias = _neighbourhood_bias(rpb_c)
    ffn1 = (w_ffn1_gate, w_ffn1_up, w_ffn1_down.astype(BF16))
    ffn2 = (w_ffn2_gate, w_ffn2_up, w_ffn2_down.astype(BF16))
    for l in range(depth):
        x = _ffn(x, norm_ffn1[l], *ffn1, l)
        x = _token_mixing(x, p, l, seqs, rope, bias, w_in, w_out)
        x = _ffn(x, norm_ffn2[l], *ffn2, l)
    y_prompt = x[:bp * sp].reshape(bp, sp, d_model)
    y_sample = x[bp * sp:].reshape(bs, ss, d_model)
    return (y_prompt, y_sample)
```

```python
import functools
import math

import numpy as np
import jax
import jax.numpy as jnp
from jax import lax
from jax.experimental import pallas as pl
from jax.experimental.pallas import tpu as pltpu

F32 = jnp.float32
BF16 = jnp.bfloat16

HEAD_DIM = 128
SUB_DIM = HEAD_DIM // 2
HEADS = 8
KV_HEADS = 2
GROUP = HEADS // KV_HEADS
GRID_W = 64
NA_KH = 8
NA_KW = 16
WINDOW = 128
ROPE_THETA = 10000.0
EPS = 1e-6
FFN_RESIDUAL = 0.5
MASK_VALUE = -1e30

QKV_BLOCKS = 72
A_Q, A_K, A_V = 0, 8, 10
B_Q, B_K, B_V = 12, 20, 22
C_Q, C_K, C_V = 24, 32, 40
D_Q, D_K, D_V = 48, 56, 64
MIX_Q = HEADS * HEAD_DIM
_MIX_A, _MIX_B, _MIX_C, _MIX_D = 0, MIX_Q, 2 * MIX_Q, 3 * MIX_Q

VMEM_CAP_BYTES = 60 * 1024 * 1024
MIB = 1024 * 1024
MXU_DIM = 256

FFN_TN = 512
FFN_PAD = 512
KEY_CHUNK = 512
FLASH_ITEMS_PER_STEP = 64
FLASH_UNROLL = 16


def _cparams(semantics, vmem_bytes):
    return pltpu.CompilerParams(
        dimension_semantics=semantics,
        vmem_limit_bytes=int(min(VMEM_CAP_BYTES, max(vmem_bytes, 16 * MIB))))


def _pick(n, candidates):
    for c in candidates:
        if n % c == 0:
            return c
    return n


def _rmsnorm_kernel(x_ref, g_ref, o_ref):
    x = x_ref[...]
    ms = jnp.mean(x * x, axis=-1, keepdims=True)
    o_ref[...] = (x * lax.rsqrt(ms + EPS) * g_ref[...]).astype(o_ref.dtype)


def _rmsnorm(x, g):
    t, d = x.shape
    tm = _pick(t, (256, 128, 64, 32, 16, 8))
    return pl.pallas_call(
        _rmsnorm_kernel,
        out_shape=jax.ShapeDtypeStruct((t, d), BF16),
        grid=(t // tm,),
        in_specs=[pl.BlockSpec((tm, d), lambda i: (i, 0)),
                  pl.BlockSpec((1, d), lambda i: (0, 0))],
        out_specs=pl.BlockSpec((tm, d), lambda i: (i, 0)),
        compiler_params=_cparams(("parallel",), 2 * tm * d * 6 + 4 * tm * d * 4),
        name="rmsnorm",
    )(x, g.reshape(1, d).astype(F32))


def _gateup_kernel(x_ref, wg_ref, wu_ref, o_ref, *, d_ff, padded):
    x = x_ref[...]
    g = jnp.dot(x, wg_ref[...].astype(BF16), preferred_element_type=F32)
    u = jnp.dot(x, wu_ref[...].astype(BF16), preferred_element_type=F32)
    h = g * (1.0 / (1.0 + jnp.exp(-g))) * u
    if padded:
        tn = o_ref.shape[1]
        col = pl.program_id(1) * tn + lax.broadcasted_iota(jnp.int32, h.shape, 1)
        h = jnp.where(col < d_ff, h, 0.0)
    o_ref[...] = h.astype(o_ref.dtype)


def _gateup(xn, wg, wu, l):
    t, d = xn.shape
    d_ff = wg.shape[2]
    tm = _pick(t, (1024, 512, 256, 128))
    tn = FFN_TN
    width = -(-d_ff // FFN_PAD) * FFN_PAD
    last = pl.cdiv(d_ff, tn) - 1
    wbytes = wg.dtype.itemsize
    vmem = (tm * d * 2 + 2 * (2 * d * tn * wbytes + tm * tn * 2) + 2 * d * tn * 2
            + 6 * tm * tn * 4)
    w_spec = pl.BlockSpec((None, d, tn), lambda i, j: (l, 0, jnp.minimum(j, last)))
    x_spec = pl.BlockSpec((tm, d), lambda i, j: (i, 0), pipeline_mode=pl.Buffered(1))
    return pl.pallas_call(
        functools.partial(_gateup_kernel, d_ff=d_ff, padded=width != d_ff),
        out_shape=jax.ShapeDtypeStruct((t, width), BF16),
        grid=(t // tm, width // tn),
        in_specs=[x_spec, w_spec, w_spec],
        out_specs=pl.BlockSpec((tm, tn), lambda i, j: (i, j)),
        compiler_params=_cparams(("parallel", "arbitrary"), vmem),
        name="ffn_gateup",
    )(xn, wg, wu)


def _mm_res_kernel(a_ref, w_ref, r_ref, o_ref, *scratch, nk, scale, k_valid):
    tk = w_ref.shape[0]

    def partial_sum(mask_tail):
        w = w_ref[...].astype(BF16)
        if mask_tail:
            row = (nk - 1) * tk + lax.broadcasted_iota(jnp.int32, w.shape, 0)
            w = jnp.where(row < k_valid, w, jnp.zeros_like(w))
        return jnp.dot(a_ref[...], w, preferred_element_type=F32)

    ragged = k_valid < nk * tk
    if nk == 1:
        o_ref[...] = r_ref[...] + scale * partial_sum(ragged)
        return
    acc_ref, = scratch
    k = pl.program_id(2)

    @pl.when(k == 0)
    def _():
        acc_ref[...] = partial_sum(False)

    @pl.when(jnp.logical_and(k > 0, k < nk - 1))
    def _():
        acc_ref[...] += partial_sum(False)

    @pl.when(k == nk - 1)
    def _():
        o_ref[...] = r_ref[...] + scale * (acc_ref[...] + partial_sum(ragged))


def _mm_res(a, w, l, res, scale, *, tn_candidates, tk_candidates):
    t, ka = a.shape
    k_valid, n = w.shape[1], w.shape[2]
    tm = _pick(t, (1024, 512, 256, 128))
    tn = _pick(n, tn_candidates)
    tk = _pick(ka, tk_candidates)
    nk = ka // tk
    assert (nk - 1) * tk < k_valid <= ka
    scratch = [pltpu.VMEM((tm, tn), F32)] if nk > 1 else []
    vmem = (2 * (tm * tk * 2 + tk * tn * w.dtype.itemsize + 2 * tm * tn * 4) + tk * tn * 2
            + 3 * tm * tn * 4)
    return pl.pallas_call(
        functools.partial(_mm_res_kernel, nk=nk, scale=scale, k_valid=k_valid),
        out_shape=jax.ShapeDtypeStruct((t, n), F32),
        grid=(t // tm, n // tn, nk),
        in_specs=[pl.BlockSpec((tm, tk), lambda i, j, k: (i, k)),
                  pl.BlockSpec((None, tk, tn), lambda i, j, k: (l, k, j)),
                  pl.BlockSpec((tm, tn), lambda i, j, k: (i, j))],
        out_specs=pl.BlockSpec((tm, tn), lambda i, j, k: (i, j)),
        scratch_shapes=scratch,
        compiler_params=_cparams(("parallel", "parallel", "arbitrary"), vmem),
        name="matmul_residual",
    )(a, w, res)


def _mm_kernel(a_ref, w_ref, o_ref):
    o_ref[...] = jnp.dot(a_ref[...], w_ref[...].astype(BF16), preferred_element_type=F32)


def _mm(a, w, l):
    t, kdim = a.shape
    n = w.shape[2]
    tm = _pick(t, (1024, 512, 256, 128))
    tn = _pick(n, (512, 256, 128))
    vmem = (2 * (tm * kdim * 2 + kdim * tn * w.dtype.itemsize + tm * tn * 4) + kdim * tn * 2
            + 2 * tm * tn * 4)
    return pl.pallas_call(
        _mm_kernel,
        out_shape=jax.ShapeDtypeStruct((t, n), F32),
        grid=(t // tm, n // tn),
        in_specs=[pl.BlockSpec((tm, kdim), lambda i, j: (i, 0)),
                  pl.BlockSpec((None, kdim, tn), lambda i, j: (l, 0, j))],
        out_specs=pl.BlockSpec((tm, tn), lambda i, j: (i, j)),
        compiler_params=_cparams(("parallel", "arbitrary"), vmem),
        name="in_proj",
    )(a, w)


_PREP_CHUNK_KINDS = (
    "a" * (HEADS + KV_HEADS) + "v" * KV_HEADS + "b" * (HEADS + KV_HEADS) + "v" * KV_HEADS
    + "c" * (2 * HEADS) + "v" * HEADS + "d" * (2 * HEADS) + "v" * HEADS)
_PREP_CHUNKS = 8
_PREP_COLS = _PREP_CHUNKS * HEAD_DIM
_PREP_BLOCK_KINDS = tuple(
    tuple(_PREP_CHUNK_KINDS[i:i + _PREP_CHUNKS]) for i in range(0, QKV_BLOCKS, _PREP_CHUNKS))
_ROPE_OFF = {"a": 0, "b": 2 * HEAD_DIM, "d": 4 * HEAD_DIM}


def _prep_kernel(p_ref, g_ref, rope_ref, o_ref):
    j = pl.program_id(1)
    tm = p_ref.shape[0]
    lane = lax.broadcasted_iota(jnp.int32, (tm, HEAD_DIM), 1)
    low_half = lane < SUB_DIM
    low_quarter = (lane & (SUB_DIM - 1)) < SUB_DIM // 2

    def chunk(c, kind):
        cols = slice(c * HEAD_DIM, (c + 1) * HEAD_DIM)
        x = p_ref[:, cols]
        if kind == "v":
            o_ref[:, cols] = x.astype(o_ref.dtype)
            return
        sq = x * x
        if kind == "d":
            lo = jnp.sum(jnp.where(low_half, sq, 0.0), axis=-1, keepdims=True)
            hi = jnp.sum(jnp.where(low_half, 0.0, sq), axis=-1, keepdims=True)
            ms = jnp.where(low_half, lo, hi) * (1.0 / SUB_DIM)
        else:
            ms = jnp.mean(sq, axis=-1, keepdims=True)
        y = x * lax.rsqrt(ms + EPS) * g_ref[:, cols]
        if kind != "c":
            off = _ROPE_OFF[kind]
            cos = rope_ref[:, off:off + HEAD_DIM]
            sin = rope_ref[:, off + HEAD_DIM:off + 2 * HEAD_DIM]
            if kind == "b":
                partner = pltpu.roll(y, SUB_DIM, 1)
            else:
                partner = jnp.where(low_quarter,
                                    pltpu.roll(y, HEAD_DIM - SUB_DIM // 2, 1),
                                    pltpu.roll(y, SUB_DIM // 2, 1))
            y = y * cos + partner * sin
        o_ref[:, cols] = y.astype(o_ref.dtype)

    groups = {}
    for jj, kinds in enumerate(_PREP_BLOCK_KINDS):
        groups.setdefault(kinds, []).append(jj)
    for kinds, js in groups.items():
        cond = functools.reduce(jnp.logical_or, [j == jj for jj in js])

        @pl.when(cond)
        def _(kinds=kinds):
            for c, kind in enumerate(kinds):
                chunk(c, kind)


def _prep(proj, gains, rope):
    t, w = proj.shape
    assert w == QKV_BLOCKS * HEAD_DIM
    tm = _pick(t, (512, 256, 128, 64, 32, 16))
    rw = rope.shape[1]
    vmem = 2 * (tm * _PREP_COLS * 6 + tm * rw * 4) + 8 * tm * _PREP_COLS * 4
    return pl.pallas_call(
        _prep_kernel,
        out_shape=jax.ShapeDtypeStruct((t, w), BF16),
        grid=(t // tm, w // _PREP_COLS),
        in_specs=[pl.BlockSpec((tm, _PREP_COLS), lambda i, j: (i, j)),
                  pl.BlockSpec((1, _PREP_COLS), lambda i, j: (0, j)),
                  pl.BlockSpec((tm, rw), lambda i, j: (i, 0))],
        out_specs=pl.BlockSpec((tm, _PREP_COLS), lambda i, j: (i, j)),
        compiler_params=_cparams(("parallel", "arbitrary"), vmem),
        name="qkv_norm_rope",
    )(proj, gains, rope)


_VT_HEADS = KV_HEADS + HEADS
_BF16_SUBLANES = 16
_VT_ROWS = HEAD_DIM + _BF16_SUBLANES


def _vt_kernel(va_ref, vd_ref, o_ref):
    ones = jnp.ones((_BF16_SUBLANES, o_ref.shape[-1]), o_ref.dtype)
    heads = ([(va_ref, h) for h in range(KV_HEADS)] + [(vd_ref, h) for h in range(HEADS)])
    for slot, (ref, h) in enumerate(heads):
        o_ref[slot, :HEAD_DIM, :] = ref[:, h * HEAD_DIM:(h + 1) * HEAD_DIM].T.astype(o_ref.dtype)
        o_ref[slot, HEAD_DIM:, :] = ones


def _values_transposed(proj):
    t = proj.shape[0]
    tk = KEY_CHUNK
    wa, wd = KV_HEADS * HEAD_DIM, HEADS * HEAD_DIM
    return pl.pallas_call(
        _vt_kernel,
        out_shape=jax.ShapeDtypeStruct((_VT_HEADS, t // tk, _VT_ROWS, tk), BF16),
        grid=(t // tk,),
        in_specs=[pl.BlockSpec((tk, wa), lambda i: (i, A_V * HEAD_DIM // wa)),
                  pl.BlockSpec((tk, wd), lambda i: (i, D_V * HEAD_DIM // wd))],
        out_specs=pl.BlockSpec((_VT_HEADS, None, _VT_ROWS, tk), lambda i: (0, i, 0, 0)),
        compiler_params=_cparams(("parallel",), 8 * tk * (wa + wd) * 4),
        name="values_transposed",
    )(proj, proj)


def _pipelined(n, produce, consume, bufs, carry, unroll):
    if n == 1:
        produce(0, bufs[0])
        return consume(0, bufs[0], carry)
    assert n % 2 == 0
    while n % unroll:
        unroll //= 2

    def group(first, carry, is_last):
        for u in range(0, unroll, 2):
            produce(first + u + 1, bufs[1])
            carry = consume(first + u, bufs[0], carry)
            if not (is_last and u == unroll - 2):
                produce(first + u + 2, bufs[0])
            carry = consume(first + u + 1, bufs[1], carry)
        return carry

    produce(0, bufs[0])
    carry = lax.fori_loop(0, n // unroll - 1,
                          lambda c, carry: group(c * unroll, carry, False), carry)
    return group(n - unroll, carry, True)


def _logits_scratch(rows, cols):
    return [pltpu.VMEM((rows, cols), F32), pltpu.VMEM((rows, cols), F32)]


def _flash_sweep(load_q_rows, n_qblocks, k_ref, vt_ref, acc_scr, q_scr, s_bufs, *, seq, rows):
    tk = KEY_CHUNK
    n = seq // tk
    acc_scr[...] = jnp.zeros(acc_scr.shape, F32)
    for qb in range(n_qblocks):
        q_scr[qb] = load_q_rows(qb).T

    def logits(t, dst):
        qb, c = lax.div(t, n), lax.rem(t, n)
        k_c = k_ref[pl.ds(pl.multiple_of(c * tk, tk), tk), :]
        dst[...] = jnp.dot(k_c, q_scr[qb], preferred_element_type=F32)

    def update(t, src, m):
        qb, c = lax.div(t, n), lax.rem(t, n)
        s = src[...]
        m = jnp.where(c == 0, -jnp.inf, m)
        m_new = jnp.maximum(m, jnp.max(s, axis=0, keepdims=True))
        alpha = jnp.exp2(m - m_new)
        p = jnp.exp2(s - m_new).astype(BF16)
        pv = jnp.dot(vt_ref[c], p, preferred_element_type=F32)
        acc_scr[qb] = alpha * acc_scr[qb] + pv
        return m_new

    _pipelined(n_qblocks * n, logits, update, s_bufs, jnp.full((1, rows), -jnp.inf, F32),
               FLASH_UNROLL)


def _flash_result(acc_scr, qb):
    acc = acc_scr[qb]
    return acc[:HEAD_DIM, :] * (1.0 / acc[HEAD_DIM:HEAD_DIM + 1, :])


def _flash_qblocks(seq, tq):
    want = max(1, FLASH_ITEMS_PER_STEP // (seq // KEY_CHUNK))
    return _pick(seq // tq, tuple(c for c in (8, 4, 2, 1) if c <= want))


def _flash_scratch(n_qblocks, rows):
    return ([pltpu.VMEM((n_qblocks, _VT_ROWS, rows), F32),
             pltpu.VMEM((n_qblocks, HEAD_DIM, rows), BF16)] + _logits_scratch(KEY_CHUNK, rows))


def _flash_vmem(seq, rows, n_qblocks):
    blocks = 2 * (seq * HEAD_DIM * 2 + seq * _VT_ROWS * 2 + 2 * n_qblocks * rows * HEAD_DIM * 2)
    return blocks + n_qblocks * rows * _VT_ROWS * 4 + 12 * rows * KEY_CHUNK * 4


def _attn_a_kernel(q_ref, k_ref, vt_ref, mixed_ref, o_ref, acc_scr, q_scr, s0_scr, s1_scr,
                   *, seq, tq, nqb):
    def load_q_rows(qb):
        q = q_ref[qb * tq:(qb + 1) * tq, :]
        return jnp.concatenate(
            [q[:, g * HEAD_DIM:(g + 1) * HEAD_DIM] for g in range(GROUP)], axis=0)

    _flash_sweep(load_q_rows, nqb, k_ref, vt_ref, acc_scr, q_scr, (s0_scr, s1_scr),
                 seq=seq, rows=GROUP * tq)
    for qb in range(nqb):
        out = _flash_result(acc_scr, qb).T
        for g in range(GROUP):
            o_ref[qb * tq:(qb + 1) * tq, g * HEAD_DIM:(g + 1) * HEAD_DIM] = (
                out[g * tq:(g + 1) * tq].astype(o_ref.dtype))


def _attn_a(qkv, vt, mixed, *, seq, nseq, tok0):
    tq = 128
    nqb = _flash_qblocks(seq, tq)
    ts = nqb * tq
    nq = seq // ts
    nc = seq // KEY_CHUNK
    qb0, sb0 = tok0 // ts, tok0 // seq
    rows = GROUP * tq
    gw = GROUP * HEAD_DIM
    return pl.pallas_call(
        functools.partial(_attn_a_kernel, seq=seq, tq=tq, nqb=nqb),
        out_shape=jax.ShapeDtypeStruct(mixed.shape, mixed.dtype),
        grid=(nseq, KV_HEADS, nq),
        in_specs=[
            pl.BlockSpec((ts, gw), lambda b, h, i: (qb0 + b * nq + i, A_Q // GROUP + h)),
            pl.BlockSpec((seq, HEAD_DIM), lambda b, h, i: (sb0 + b, A_K + h)),
            pl.BlockSpec((None, nc, _VT_ROWS, KEY_CHUNK), lambda b, h, i: (h, sb0 + b, 0, 0)),
            pl.BlockSpec(memory_space=pl.ANY),
        ],
        out_specs=pl.BlockSpec((ts, gw), lambda b, h, i: (qb0 + b * nq + i, _MIX_A // gw + h)),
        input_output_aliases={3: 0},
        scratch_shapes=_flash_scratch(nqb, rows),
        compiler_params=_cparams(("parallel", "parallel", "arbitrary"),
                                 _flash_vmem(seq, rows, nqb)),
        name="attn_global_gqa",
    )(qkv, qkv, vt, mixed)


def _attn_d_kernel(q_ref, k_ref, vt_ref, lam_ref, subln_ref, mixed_ref, o_ref, acc_scr, q_scr,
                   s0_scr, s1_scr, *, seq, tq, nqb, lambda_init):
    lane = lax.broadcasted_iota(jnp.int32, (tq, HEAD_DIM), 1)

    def load_q_rows(qb):
        q = q_ref[qb * tq:(qb + 1) * tq, :]
        zero = jnp.zeros_like(q)
        return jnp.concatenate([jnp.where(lane < SUB_DIM, q, zero),
                                jnp.where(lane < SUB_DIM, zero, q)], axis=0)

    _flash_sweep(load_q_rows, nqb, k_ref, vt_ref, acc_scr, q_scr, (s0_scr, s1_scr),
                 seq=seq, rows=2 * tq)
    lp = lam_ref[...]
    lam = (jnp.exp(jnp.sum(lp[0:1] * lp[1:2], axis=1, keepdims=True))
           - jnp.exp(jnp.sum(lp[2:3] * lp[3:4], axis=1, keepdims=True)) + lambda_init)
    for qb in range(nqb):
        out_t = _flash_result(acc_scr, qb)
        d = (out_t[:, :tq] - lam * out_t[:, tq:]).T
        ms = jnp.mean(d * d, axis=-1, keepdims=True)
        y = d * lax.rsqrt(ms + EPS) * subln_ref[...] * (1.0 - lambda_init)
        o_ref[qb * tq:(qb + 1) * tq, :] = y.astype(o_ref.dtype)


def _attn_d(qkv, vt, lam_params, subln, mixed, *, seq, nseq, tok0, lambda_init):
    tq = _pick(seq, (256, 128))
    nqb = _flash_qblocks(seq, tq)
    ts = nqb * tq
    nq = seq // ts
    nc = seq // KEY_CHUNK
    qb0, sb0 = tok0 // ts, tok0 // seq
    rows = 2 * tq
    return pl.pallas_call(
        functools.partial(_attn_d_kernel, seq=seq, tq=tq, nqb=nqb, lambda_init=lambda_init),
        out_shape=jax.ShapeDtypeStruct(mixed.shape, mixed.dtype),
        grid=(nseq, HEADS, nq),
        in_specs=[
            pl.BlockSpec((ts, HEAD_DIM), lambda b, h, i: (qb0 + b * nq + i, D_Q + h)),
            pl.BlockSpec((seq, HEAD_DIM), lambda b, h, i: (sb0 + b, D_K + h)),
            pl.BlockSpec((None, nc, _VT_ROWS, KEY_CHUNK),
                         lambda b, h, i: (KV_HEADS + h, sb0 + b, 0, 0)),
            pl.BlockSpec((4, SUB_DIM), lambda b, h, i: (0, 0)),
            pl.BlockSpec((1, HEAD_DIM), lambda b, h, i: (0, 0)),
            pl.BlockSpec(memory_space=pl.ANY),
        ],
        out_specs=pl.BlockSpec((ts, HEAD_DIM),
                               lambda b, h, i: (qb0 + b * nq + i, _MIX_D // HEAD_DIM + h)),
        input_output_aliases={5: 0},
        scratch_shapes=_flash_scratch(nqb, rows),
        compiler_params=_cparams(("parallel", "parallel", "arbitrary"),
                                 _flash_vmem(seq, rows, nqb)),
        name="attn_differential",
    )(qkv, qkv, vt, lam_params, subln, mixed)


_B_KEYS = 3 * WINDOW
_B_UNROLL = 2


def _window_mask_table():
    o = np.arange(3)[:, None, None] * WINDOW
    r = np.arange(WINDOW)[None, :, None]
    c = np.arange(_B_KEYS)[None, None, :]
    return jnp.asarray(np.where(np.abs(o + r - c) <= WINDOW, 0.0, MASK_VALUE), F32)


def _attn_b_kernel(sink_ref, q_ref, k_ref, v_ref, mask_ref, mixed_ref, o_ref, s0_scr, s1_scr,
                   *, seq, blocks_per_step):
    h = pl.program_id(1)
    step = pl.program_id(2)
    tq = WINDOW
    sink = jnp.concatenate(
        [jnp.full((tq, 1), sink_ref[h * GROUP + g], F32) for g in range(GROUP)], axis=0)

    def band_start(u):
        n = step * blocks_per_step + u
        kstart = jnp.clip((n - 1) * tq, 0, seq - _B_KEYS)
        return pl.multiple_of(kstart, tq), (n * tq - kstart) // tq

    def logits(u, dst):
        kstart, _ = band_start(u)
        q = q_ref[pl.ds(pl.multiple_of(u * tq, tq), tq), :]
        q_rows = jnp.concatenate(
            [q[:, g * HEAD_DIM:(g + 1) * HEAD_DIM] for g in range(GROUP)], axis=0)
        dst[...] = lax.dot_general(q_rows, k_ref[pl.ds(kstart, _B_KEYS), :],
                                   (((1,), (1,)), ((), ())),
                                   preferred_element_type=F32)

    def attend(u, src, carry):
        kstart, offset = band_start(u)
        mask = mask_ref[offset]
        s = src[...] + jnp.concatenate([mask] * GROUP, axis=0)
        m = jnp.maximum(jnp.max(s, axis=1, keepdims=True), sink)
        p = jnp.exp(s - m)
        denom = jnp.sum(p, axis=1, keepdims=True) + jnp.exp(sink - m)
        out = jnp.dot(p.astype(BF16), v_ref[pl.ds(kstart, _B_KEYS), :],
                      preferred_element_type=F32) / denom
        rows = pl.ds(pl.multiple_of(u * tq, tq), tq)
        for g in range(GROUP):
            o_ref[rows, g * HEAD_DIM:(g + 1) * HEAD_DIM] = (
                out[g * tq:(g + 1) * tq].astype(o_ref.dtype))
        return carry

    _pipelined(blocks_per_step, logits, attend, (s0_scr, s1_scr), 0, _B_UNROLL)


def _attn_b(qkv, sink, mixed, *, seq, nseq, tok0):
    nblk = _pick(seq // WINDOW, (16, 8, 4, 2))
    ts = nblk * WINDOW
    assert seq >= _B_KEYS and seq % ts == 0
    nq = seq // ts
    qb0, sb0 = tok0 // ts, tok0 // seq
    gw = GROUP * HEAD_DIM
    rows = GROUP * WINDOW
    vmem = (2 * (2 * seq * HEAD_DIM * 2 + 2 * ts * gw * 2) + 3 * WINDOW * _B_KEYS * 4
            + 10 * rows * _B_KEYS * 4)
    return pl.pallas_call(
        functools.partial(_attn_b_kernel, seq=seq, blocks_per_step=nblk),
        out_shape=jax.ShapeDtypeStruct(mixed.shape, mixed.dtype),
        grid_spec=pltpu.PrefetchScalarGridSpec(
            num_scalar_prefetch=1,
            grid=(nseq, KV_HEADS, nq),
            in_specs=[
                pl.BlockSpec((ts, gw), lambda b, h, i, s_: (qb0 + b * nq + i, B_Q // GROUP + h)),
                pl.BlockSpec((seq, HEAD_DIM), lambda b, h, i, s_: (sb0 + b, B_K + h)),
                pl.BlockSpec((seq, HEAD_DIM), lambda b, h, i, s_: (sb0 + b, B_V + h)),
                pl.BlockSpec((3, WINDOW, _B_KEYS), lambda b, h, i, s_: (0, 0, 0)),
                pl.BlockSpec(memory_space=pl.ANY),
            ],
            out_specs=pl.BlockSpec(
                (ts, gw), lambda b, h, i, s_: (qb0 + b * nq + i, _MIX_B // gw + h)),
            scratch_shapes=_logits_scratch(rows, _B_KEYS),
        ),
        input_output_aliases={5: 0},
        compiler_params=_cparams(("parallel", "parallel", "arbitrary"), vmem),
        name="attn_window_sink",
    )(sink, qkv, qkv, qkv, _window_mask_table(), mixed)


_C_KEYS = NA_KH * GRID_W
_C_BIAS_ROWS = 2 * NA_KH - 2
_C_ROWS_PER_STAGE = 4
_C_UNROLL = 8


def _attn_c_kernel(q_ref, k_ref, v_ref, bias_ref, mixed_ref, o_ref, s0_scr, s1_scr,
                   *, grid_rows, rows_per_step):
    i = pl.program_id(2)
    stage_rows = _C_ROWS_PER_STAGE * GRID_W

    def window(st, t):
        r = i * rows_per_step + st * _C_ROWS_PER_STAGE + t
        rs = jnp.clip(r - NA_KH // 2, 0, grid_rows - NA_KH)
        return pl.multiple_of(rs * GRID_W, GRID_W), rs - r + NA_KH - 1

    def logits(st, dst):
        for t in range(_C_ROWS_PER_STAGE):
            kstart, _ = window(st, t)
            q = q_ref[pl.ds(pl.multiple_of(st * stage_rows + t * GRID_W, GRID_W), GRID_W), :]
            dst[t * GRID_W:(t + 1) * GRID_W, :] = lax.dot_general(
                q, k_ref[pl.ds(kstart, _C_KEYS), :], (((1,), (1,)), ((), ())),
                preferred_element_type=F32)

    def attend(st, src, carry):
        wins = [window(st, t) for t in range(_C_ROWS_PER_STAGE)]
        bias = jnp.concatenate(
            [jnp.concatenate([bias_ref[dr0 + 2 * u] for u in range(NA_KH // 2)], axis=1)
             for _, dr0 in wins], axis=0)
        s = src[...] + bias
        m = jnp.max(s, axis=1, keepdims=True)
        p = jnp.exp(s - m)
        denom = jnp.sum(p, axis=1, keepdims=True)
        pb = p.astype(BF16)
        out = jnp.concatenate(
            [jnp.dot(pb[t * GRID_W:(t + 1) * GRID_W], v_ref[pl.ds(kstart, _C_KEYS), :],
                     preferred_element_type=F32) for t, (kstart, _) in enumerate(wins)], axis=0)
        rows = pl.ds(pl.multiple_of(st * stage_rows, stage_rows), stage_rows)
        o_ref[rows, :] = (out / denom).astype(o_ref.dtype)
        return carry

    _pipelined(rows_per_step // _C_ROWS_PER_STAGE, logits, attend, (s0_scr, s1_scr), 0,
               _C_UNROLL)


def _neighbourhood_bias(rpb):
    c = np.arange(GRID_W)[:, None]
    kc = np.arange(GRID_W)[None, :]
    cs = np.clip(c - NA_KW // 2, 0, GRID_W - NA_KW)
    inside = (kc >= cs) & (kc < cs + NA_KW)
    dc = np.clip(kc - c + NA_KW - 1, 0, 2 * NA_KW - 2)
    select = np.zeros((2 * NA_KW - 1, GRID_W * GRID_W), np.float32)
    select[dc.reshape(-1), np.arange(GRID_W * GRID_W)] = 1.0
    base = jnp.einsum("lhdj,jx->lhdx", rpb.astype(F32), jnp.asarray(select),
                      precision=lax.Precision.HIGHEST)
    base = base.reshape(rpb.shape[:3] + (GRID_W, GRID_W))
    base = jnp.where(inside, base, MASK_VALUE)
    return jnp.concatenate([base[:, :, :-1], base[:, :, 1:]], axis=-1)


def _attn_c(qkv, bias, l, mixed, *, seq, nseq, tok0):
    grid_rows = seq // GRID_W
    assert grid_rows >= NA_KH
    rows_per_step = _pick(grid_rows, (32, 16, 8))
    tq = rows_per_step * GRID_W
    nq = seq // tq
    qb0, sb0 = tok0 // tq, tok0 // seq
    vmem = (2 * (2 * seq * HEAD_DIM * 2 + 2 * tq * HEAD_DIM * 2
                 + _C_BIAS_ROWS * GRID_W * 2 * GRID_W * 4)
            + 12 * _C_ROWS_PER_STAGE * GRID_W * _C_KEYS * 4)
    return pl.pallas_call(
        functools.partial(_attn_c_kernel, grid_rows=grid_rows, rows_per_step=rows_per_step),
        out_shape=jax.ShapeDtypeStruct(mixed.shape, mixed.dtype),
        grid=(nseq, HEADS, nq),
        in_specs=[
            pl.BlockSpec((tq, HEAD_DIM), lambda b, h, i: (qb0 + b * nq + i, C_Q + h)),
            pl.BlockSpec((seq, HEAD_DIM), lambda b, h, i: (sb0 + b, C_K + h)),
            pl.BlockSpec((seq, HEAD_DIM), lambda b, h, i: (sb0 + b, C_V + h)),
            pl.BlockSpec((None, None, _C_BIAS_ROWS, GRID_W, 2 * GRID_W),
                         lambda b, h, i: (l, h, 0, 0, 0)),
            pl.BlockSpec(memory_space=pl.ANY),
        ],
        out_specs=pl.BlockSpec((tq, HEAD_DIM),
                               lambda b, h, i: (qb0 + b * nq + i, _MIX_C // HEAD_DIM + h)),
        input_output_aliases={4: 0},
        scratch_shapes=_logits_scratch(_C_ROWS_PER_STAGE * GRID_W, _C_KEYS),
        compiler_params=_cparams(("parallel", "parallel", "arbitrary"), vmem),
        name="attn_neighbourhood",
    )(qkv, qkv, qkv, bias, mixed)


def _rope_tables(seqs):
    pos = np.concatenate([np.tile(np.arange(s), n) for s, n in seqs]).astype(np.int32)
    pos = jnp.asarray(pos)
    row = (pos // GRID_W).astype(F32)[:, None]
    col = (pos % GRID_W).astype(F32)[:, None]
    t = pos.astype(F32)[:, None]
    inv64 = ROPE_THETA ** (-jnp.arange(0, SUB_DIM, 2, dtype=F32) / SUB_DIM)[None, :]
    inv128 = ROPE_THETA ** (-jnp.arange(0, HEAD_DIM, 2, dtype=F32) / HEAD_DIM)[None, :]

    def pair(ang):
        c, s = jnp.cos(ang), jnp.sin(ang)
        return jnp.concatenate([c, c], axis=1), jnp.concatenate([-s, s], axis=1)

    cr, sr = pair(row * inv64)
    cc, sc = pair(col * inv64)
    cb, sb = pair(t * inv128)
    cd, sd = pair(t * inv64)
    return jnp.concatenate([cr, cc, sr, sc, cb, sb, cd, cd, sd, sd], axis=1)


def _qkv_gains(p, l):
    scale = HEAD_DIM ** -0.5
    scale_d = SUB_DIM ** -0.5
    log2e = math.log2(math.e)
    ones = lambda n: jnp.ones((n * HEAD_DIM,), F32)
    tile = lambda g, n: jnp.tile(g.astype(F32), n)
    parts = [
        tile(p["q_norm_a"][l], HEADS) * (scale * log2e), tile(p["k_norm_a"][l], KV_HEADS),
        ones(KV_HEADS),
        tile(p["q_norm_b"][l], HEADS) * scale, tile(p["k_norm_b"][l], KV_HEADS), ones(KV_HEADS),
        tile(p["q_norm_c"][l], HEADS) * scale, tile(p["k_norm_c"][l], HEADS), ones(HEADS),
        tile(p["q_norm_d"][l], 2 * HEADS) * (scale_d * log2e), tile(p["k_norm_d"][l], 2 * HEADS),
        ones(HEADS),
    ]
    return jnp.concatenate(parts).reshape(1, QKV_BLOCKS * HEAD_DIM)


def _ffn(x, g, wg, wu, wd, l):
    xn = _rmsnorm(x, g)
    hidden = _gateup(xn, wg, wu, l)
    tk_candidates = tuple(m * MXU_DIM for m in (11, 8, 4, 2, 1))
    return _mm_res(hidden, wd, l, x, FFN_RESIDUAL, tn_candidates=(1024, 512, 256, 128),
                   tk_candidates=tk_candidates)


def _token_mixing(x, p, l, seqs, rope, bias, w_in, w_out):
    lambda_init = 0.8 - 0.6 * math.exp(-0.3 * l)
    xn = _rmsnorm(x, p["norm_mix"][l])
    proj = _mm(xn, w_in, l)
    qkv = _prep(proj, _qkv_gains(p, l), rope)
    vt = _values_transposed(proj)
    sink = p["sink_b"][l].astype(F32)
    lam_params = jnp.stack([p["lambda_q1"][l], p["lambda_k1"][l],
                            p["lambda_q2"][l], p["lambda_k2"][l]]).astype(F32)
    subln = p["subln_d"][l].astype(F32).reshape(1, HEAD_DIM)
    mixed = jnp.zeros((x.shape[0], 4 * MIX_Q), BF16)
    tok0 = 0
    for seq, nseq in seqs:
        kw = dict(seq=seq, nseq=nseq, tok0=tok0)
        mixed = _attn_a(qkv, vt, mixed, **kw)
        mixed = _attn_b(qkv, sink, mixed, **kw)
        mixed = _attn_c(qkv, bias, l, mixed, **kw)
        mixed = _attn_d(qkv, vt, lam_params, subln, mixed, lambda_init=lambda_init, **kw)
        tok0 += seq * nseq
    return _mm_res(mixed, w_out, l, x, 1.0, tn_candidates=(512, 256, 128),
                   tk_candidates=(mixed.shape[1],))


def kernel(x_prompt, x_sample, norm_ffn1, w_ffn1_gate, w_ffn1_up, w_ffn1_down, norm_mix, w_in,
           q_norm_a, k_norm_a, q_norm_b, k_norm_b, sink_b, q_norm_c, k_norm_c, rpb_c,
           q_norm_d, k_norm_d, lambda_q1, lambda_k1, lambda_q2, lambda_k2, subln_d, w_out,
           norm_ffn2, w_ffn2_gate, w_ffn2_up, w_ffn2_down):
    p = dict(norm_mix=norm_mix, q_norm_a=q_norm_a, k_norm_a=k_norm_a, q_norm_b=q_norm_b,
             k_norm_b=k_norm_b, sink_b=sink_b, q_norm_c=q_norm_c, k_norm_c=k_norm_c, rpb_c=rpb_c,
             q_norm_d=q_norm_d, k_norm_d=k_norm_d, lambda_q1=lambda_q1, lambda_k1=lambda_k1,
             lambda_q2=lambda_q2, lambda_k2=lambda_k2, subln_d=subln_d)
    depth = w_in.shape[0]
    d_model = x_prompt.shape[-1]
    bp, sp, _ = x_prompt.shape
    bs, ss, _ = x_sample.shape
    seqs = ((sp, bp), (ss, bs))
    assert (bp * sp) % ss == 0
    x = jnp.concatenate([x_prompt.reshape(bp * sp, d_model),
                         x_sample.reshape(bs * ss, d_model)], axis=0)
    rope = _rope_tables(seqs)
    bias = _neighbourhood_bias(rpb_c)
    ffn1 = (w_ffn1_gate, w_ffn1_up, w_ffn1_down.astype(BF16))
    ffn2 = (w_ffn2_gate, w_ffn2_up, w_ffn2_down.astype(BF16))
    for l in range(depth):
        x = _ffn(x, norm_ffn1[l], *ffn1, l)
        x = _token_mixing(x, p, l, seqs, rope, bias, w_in, w_out)
        x = _ffn(x, norm_ffn2[l], *ffn2, l)
    y_prompt = x[:bp * sp].reshape(bp, sp, d_model)
    y_sample = x[bp * sp:].reshape(bs, ss, d_model)
    return (y_prompt, y_sample)
```

```python
import functools
import math

import numpy as np
import jax
import jax.numpy as jnp
from jax import lax
from jax.experimental import pallas as pl
from jax.experimental.pallas import tpu as pltpu

F32 = jnp.float32
BF16 = jnp.bfloat16

HEAD_DIM = 128
SUB_DIM = HEAD_DIM // 2
HEADS = 8
KV_HEADS = 2
GROUP = HEADS // KV_HEADS
GRID_W = 64
NA_KH = 8
NA_KW = 16
WINDOW = 128
ROPE_THETA = 10000.0
EPS = 1e-6
FFN_RESIDUAL = 0.5
MASK_VALUE = -1e30

QKV_BLOCKS = 72
A_Q, A_K, A_V = 0, 8, 10
B_Q, B_K, B_V = 12, 20, 22
C_Q, C_K, C_V = 24, 32, 40
D_Q, D_K, D_V = 48, 56, 64
MIX_Q = HEADS * HEAD_DIM
_MIX_A, _MIX_B, _MIX_C, _MIX_D = 0, MIX_Q, 2 * MIX_Q, 3 * MIX_Q

VMEM_CAP_BYTES = 60 * 1024 * 1024
MIB = 1024 * 1024
MXU_DIM = 256

FFN_TN = 512
FFN_PAD = 512
KEY_CHUNK = 512
FLASH_ITEMS_PER_STEP = 128
FLASH_UNROLL = 16


def _cparams(semantics, vmem_bytes):
    return pltpu.CompilerParams(
        dimension_semantics=semantics,
        vmem_limit_bytes=int(min(VMEM_CAP_BYTES, max(vmem_bytes, 16 * MIB))))


def _pick(n, candidates):
    for c in candidates:
        if n % c == 0:
            return c
    return n


def _rmsnorm_kernel(x_ref, g_ref, o_ref):
    x = x_ref[...]
    ms = jnp.mean(x * x, axis=-1, keepdims=True)
    o_ref[...] = (x * lax.rsqrt(ms + EPS) * g_ref[...]).astype(o_ref.dtype)


def _rmsnorm(x, g):
    t, d = x.shape
    tm = _pick(t, (512, 256, 128, 64, 32, 16, 8))
    return pl.pallas_call(
        _rmsnorm_kernel,
        out_shape=jax.ShapeDtypeStruct((t, d), BF16),
        grid=(t // tm,),
        in_specs=[pl.BlockSpec((tm, d), lambda i: (i, 0)),
                  pl.BlockSpec((1, d), lambda i: (0, 0))],
        out_specs=pl.BlockSpec((tm, d), lambda i: (i, 0)),
        compiler_params=_cparams(("parallel",), 2 * tm * d * 6 + 4 * tm * d * 4),
        name="rmsnorm",
    )(x, g.reshape(1, d).astype(F32))


def _gateup_kernel(x_ref, wg_ref, wu_ref, o_ref, *, d_ff, padded):
    x = x_ref[...]
    g = jnp.dot(x, wg_ref[...].astype(BF16), preferred_element_type=F32)
    u = jnp.dot(x, wu_ref[...].astype(BF16), preferred_element_type=F32)
    h = g * (1.0 / (1.0 + jnp.exp(-g))) * u
    if padded:
        tn = o_ref.shape[1]
        col = pl.program_id(1) * tn + lax.broadcasted_iota(jnp.int32, h.shape, 1)
        h = jnp.where(col < d_ff, h, 0.0)
    o_ref[...] = h.astype(o_ref.dtype)


def _gateup(xn, wg, wu, l):
    t, d = xn.shape
    d_ff = wg.shape[2]
    tm = _pick(t, (1024, 512, 256, 128))
    tn = FFN_TN
    width = -(-d_ff // FFN_PAD) * FFN_PAD
    last = pl.cdiv(d_ff, tn) - 1
    wbytes = wg.dtype.itemsize
    vmem = (tm * d * 2 + 2 * (2 * d * tn * wbytes + tm * tn * 2) + 2 * d * tn * 2
            + 6 * tm * tn * 4)
    w_spec = pl.BlockSpec((None, d, tn), lambda i, j: (l, 0, jnp.minimum(j, last)))
    x_spec = pl.BlockSpec((tm, d), lambda i, j: (i, 0), pipeline_mode=pl.Buffered(1))
    return pl.pallas_call(
        functools.partial(_gateup_kernel, d_ff=d_ff, padded=width != d_ff),
        out_shape=jax.ShapeDtypeStruct((t, width), BF16),
        grid=(t // tm, width // tn),
        in_specs=[x_spec, w_spec, w_spec],
        out_specs=pl.BlockSpec((tm, tn), lambda i, j: (i, j)),
        compiler_params=_cparams(("parallel", "arbitrary"), vmem),
        name="ffn_gateup",
    )(xn, wg, wu)


def _mm_res_kernel(a_ref, w_ref, r_ref, o_ref, *scratch, nk, scale, k_valid):
    tk = w_ref.shape[0]

    def partial_sum(mask_tail):
        w = w_ref[...].astype(BF16)
        if mask_tail:
            row = (nk - 1) * tk + lax.broadcasted_iota(jnp.int32, w.shape, 0)
            w = jnp.where(row < k_valid, w, jnp.zeros_like(w))
        return jnp.dot(a_ref[...], w, preferred_element_type=F32)

    ragged = k_valid < nk * tk
    if nk == 1:
        o_ref[...] = r_ref[...] + scale * partial_sum(ragged)
        return
    acc_ref, = scratch
    k = pl.program_id(2)

    @pl.when(k == 0)
    def _():
        acc_ref[...] = partial_sum(False)

    @pl.when(jnp.logical_and(k > 0, k < nk - 1))
    def _():
        acc_ref[...] += partial_sum(False)

    @pl.when(k == nk - 1)
    def _():
        o_ref[...] = r_ref[...] + scale * (acc_ref[...] + partial_sum(ragged))


def _mm_res(a, w, l, res, scale, *, tn_candidates, tk_candidates):
    t, ka = a.shape
    k_valid, n = w.shape[1], w.shape[2]
    tm = _pick(t, (1024, 512, 256, 128))
    tn = _pick(n, tn_candidates)
    tk = _pick(ka, tk_candidates)
    nk = ka // tk
    assert (nk - 1) * tk < k_valid <= ka
    scratch = [pltpu.VMEM((tm, tn), F32)] if nk > 1 else []
    vmem = (2 * (tm * tk * 2 + tk * tn * w.dtype.itemsize + 2 * tm * tn * 4) + tk * tn * 2
            + 3 * tm * tn * 4)
    return pl.pallas_call(
        functools.partial(_mm_res_kernel, nk=nk, scale=scale, k_valid=k_valid),
        out_shape=jax.ShapeDtypeStruct((t, n), F32),
        grid=(t // tm, n // tn, nk),
        in_specs=[pl.BlockSpec((tm, tk), lambda i, j, k: (i, k)),
                  pl.BlockSpec((None, tk, tn), lambda i, j, k: (l, k, j)),
                  pl.BlockSpec((tm, tn), lambda i, j, k: (i, j))],
        out_specs=pl.BlockSpec((tm, tn), lambda i, j, k: (i, j)),
        scratch_shapes=scratch,
        compiler_params=_cparams(("parallel", "parallel", "arbitrary"), vmem),
        name="matmul_residual",
    )(a, w, res)


def _mm_kernel(a_ref, w_ref, o_ref):
    o_ref[...] = jnp.dot(a_ref[...], w_ref[...].astype(BF16), preferred_element_type=F32)


def _mm(a, w, l):
    t, kdim = a.shape
    n = w.shape[2]
    tm = _pick(t, (1024, 512, 256, 128))
    tn = _pick(n, (512, 256, 128))
    vmem = (2 * (tm * kdim * 2 + kdim * tn * w.dtype.itemsize + tm * tn * 4) + kdim * tn * 2
            + 2 * tm * tn * 4)
    return pl.pallas_call(
        _mm_kernel,
        out_shape=jax.ShapeDtypeStruct((t, n), F32),
        grid=(t // tm, n // tn),
        in_specs=[pl.BlockSpec((tm, kdim), lambda i, j: (i, 0)),
                  pl.BlockSpec((None, kdim, tn), lambda i, j: (l, 0, j))],
        out_specs=pl.BlockSpec((tm, tn), lambda i, j: (i, j)),
        compiler_params=_cparams(("parallel", "arbitrary"), vmem),
        name="in_proj",
    )(a, w)


_PREP_CHUNK_KINDS = (
    "a" * (HEADS + KV_HEADS) + "v" * KV_HEADS + "b" * (HEADS + KV_HEADS) + "v" * KV_HEADS
    + "c" * (2 * HEADS) + "v" * HEADS + "d" * (2 * HEADS) + "v" * HEADS)
_PREP_CHUNKS = 8
_PREP_COLS = _PREP_CHUNKS * HEAD_DIM
_PREP_BLOCK_KINDS = tuple(
    tuple(_PREP_CHUNK_KINDS[i:i + _PREP_CHUNKS]) for i in range(0, QKV_BLOCKS, _PREP_CHUNKS))
_ROPE_OFF = {"a": 0, "b": 2 * HEAD_DIM, "d": 4 * HEAD_DIM}


def _prep_kernel(p_ref, g_ref, rope_ref, o_ref):
    j = pl.program_id(1)
    tm = p_ref.shape[0]
    lane = lax.broadcasted_iota(jnp.int32, (tm, HEAD_DIM), 1)
    low_half = lane < SUB_DIM
    low_quarter = (lane & (SUB_DIM - 1)) < SUB_DIM // 2

    def chunk(c, kind):
        cols = slice(c * HEAD_DIM, (c + 1) * HEAD_DIM)
        x = p_ref[:, cols]
        if kind == "v":
            o_ref[:, cols] = x.astype(o_ref.dtype)
            return
        sq = x * x
        if kind == "d":
            lo = jnp.sum(jnp.where(low_half, sq, 0.0), axis=-1, keepdims=True)
            hi = jnp.sum(jnp.where(low_half, 0.0, sq), axis=-1, keepdims=True)
            ms = jnp.where(low_half, lo, hi) * (1.0 / SUB_DIM)
        else:
            ms = jnp.mean(sq, axis=-1, keepdims=True)
        y = x * lax.rsqrt(ms + EPS) * g_ref[:, cols]
        if kind != "c":
            off = _ROPE_OFF[kind]
            cos = rope_ref[:, off:off + HEAD_DIM]
            sin = rope_ref[:, off + HEAD_DIM:off + 2 * HEAD_DIM]
            if kind == "b":
                partner = pltpu.roll(y, SUB_DIM, 1)
            else:
                partner = jnp.where(low_quarter,
                                    pltpu.roll(y, HEAD_DIM - SUB_DIM // 2, 1),
                                    pltpu.roll(y, SUB_DIM // 2, 1))
            y = y * cos + partner * sin
        o_ref[:, cols] = y.astype(o_ref.dtype)

    groups = {}
    for jj, kinds in enumerate(_PREP_BLOCK_KINDS):
        groups.setdefault(kinds, []).append(jj)
    for kinds, js in groups.items():
        cond = functools.reduce(jnp.logical_or, [j == jj for jj in js])

        @pl.when(cond)
        def _(kinds=kinds):
            for c, kind in enumerate(kinds):
                chunk(c, kind)


def _prep(proj, gains, rope):
    t, w = proj.shape
    assert w == QKV_BLOCKS * HEAD_DIM
    tm = _pick(t, (1024, 512, 256, 128, 64, 32, 16))
    rw = rope.shape[1]
    vmem = 2 * (tm * _PREP_COLS * 6 + tm * rw * 4) + 8 * tm * _PREP_COLS * 4
    return pl.pallas_call(
        _prep_kernel,
        out_shape=jax.ShapeDtypeStruct((t, w), BF16),
        grid=(t // tm, w // _PREP_COLS),
        in_specs=[pl.BlockSpec((tm, _PREP_COLS), lambda i, j: (i, j)),
                  pl.BlockSpec((1, _PREP_COLS), lambda i, j: (0, j)),
                  pl.BlockSpec((tm, rw), lambda i, j: (i, 0))],
        out_specs=pl.BlockSpec((tm, _PREP_COLS), lambda i, j: (i, j)),
        compiler_params=_cparams(("parallel", "arbitrary"), vmem),
        name="qkv_norm_rope",
    )(proj, gains, rope)


_VT_HEADS = KV_HEADS + HEADS
_BF16_SUBLANES = 16
_VT_ROWS = HEAD_DIM + _BF16_SUBLANES


def _vt_kernel(va_ref, vd_ref, o_ref):
    ones = jnp.ones((_BF16_SUBLANES, o_ref.shape[-1]), o_ref.dtype)
    heads = ([(va_ref, h) for h in range(KV_HEADS)] + [(vd_ref, h) for h in range(HEADS)])
    for slot, (ref, h) in enumerate(heads):
        o_ref[slot, :HEAD_DIM, :] = ref[:, h * HEAD_DIM:(h + 1) * HEAD_DIM].T.astype(o_ref.dtype)
        o_ref[slot, HEAD_DIM:, :] = ones


def _values_transposed(proj):
    t = proj.shape[0]
    tk = KEY_CHUNK
    wa, wd = KV_HEADS * HEAD_DIM, HEADS * HEAD_DIM
    return pl.pallas_call(
        _vt_kernel,
        out_shape=jax.ShapeDtypeStruct((_VT_HEADS, t // tk, _VT_ROWS, tk), BF16),
        grid=(t // tk,),
        in_specs=[pl.BlockSpec((tk, wa), lambda i: (i, A_V * HEAD_DIM // wa)),
                  pl.BlockSpec((tk, wd), lambda i: (i, D_V * HEAD_DIM // wd))],
        out_specs=pl.BlockSpec((_VT_HEADS, None, _VT_ROWS, tk), lambda i: (0, i, 0, 0)),
        compiler_params=_cparams(("parallel",), 8 * tk * (wa + wd) * 4),
        name="values_transposed",
    )(proj, proj)


def _pipelined(n, produce, consume, bufs, carry, unroll):
    if n == 1:
        produce(0, bufs[0])
        return consume(0, bufs[0], carry)
    assert n % 2 == 0
    while n % unroll:
        unroll //= 2

    def group(first, carry, is_last):
        for u in range(0, unroll, 2):
            produce(first + u + 1, bufs[1])
            carry = consume(first + u, bufs[0], carry)
            if not (is_last and u == unroll - 2):
                produce(first + u + 2, bufs[0])
            carry = consume(first + u + 1, bufs[1], carry)
        return carry

    produce(0, bufs[0])
    carry = lax.fori_loop(0, n // unroll - 1,
                          lambda c, carry: group(c * unroll, carry, False), carry)
    return group(n - unroll, carry, True)


def _logits_scratch(rows, cols):
    return [pltpu.VMEM((rows, cols), F32), pltpu.VMEM((rows, cols), F32)]


def _flash_sweep(load_q_rows, n_qblocks, k_ref, vt_ref, acc_scr, q_scr, s_bufs, *, seq, rows):
    tk = KEY_CHUNK
    n = seq // tk
    acc_scr[...] = jnp.zeros(acc_scr.shape, F32)
    for qb in range(n_qblocks):
        q_scr[qb] = load_q_rows(qb).T

    def logits(t, dst):
        qb, c = lax.div(t, n), lax.rem(t, n)
        k_c = k_ref[pl.ds(pl.multiple_of(c * tk, tk), tk), :]
        dst[...] = jnp.dot(k_c, q_scr[qb], preferred_element_type=F32)

    def update(t, src, m):
        qb, c = lax.div(t, n), lax.rem(t, n)
        s = src[...]
        m = jnp.where(c == 0, -jnp.inf, m)
        m_new = jnp.maximum(m, jnp.max(s, axis=0, keepdims=True))
        alpha = jnp.exp2(m - m_new)
        p = jnp.exp2(s - m_new).astype(BF16)
        pv = jnp.dot(vt_ref[c], p, preferred_element_type=F32)
        acc_scr[qb] = alpha * acc_scr[qb] + pv
        return m_new

    _pipelined(n_qblocks * n, logits, update, s_bufs, jnp.full((1, rows), -jnp.inf, F32),
               FLASH_UNROLL)


def _flash_result(acc_scr, qb):
    acc = acc_scr[qb]
    return acc[:HEAD_DIM, :] * (1.0 / acc[HEAD_DIM:HEAD_DIM + 1, :])


def _flash_qblocks(seq, tq):
    want = max(1, FLASH_ITEMS_PER_STEP // (seq // KEY_CHUNK))
    return _pick(seq // tq, tuple(c for c in (8, 4, 2, 1) if c <= want))


def _flash_scratch(n_qblocks, rows):
    return ([pltpu.VMEM((n_qblocks, _VT_ROWS, rows), F32),
             pltpu.VMEM((n_qblocks, HEAD_DIM, rows), BF16)] + _logits_scratch(KEY_CHUNK, rows))


def _flash_vmem(seq, rows, n_qblocks):
    blocks = 2 * (seq * HEAD_DIM * 2 + seq * _VT_ROWS * 2 + 2 * n_qblocks * rows * HEAD_DIM * 2)
    return blocks + n_qblocks * rows * _VT_ROWS * 4 + 12 * rows * KEY_CHUNK * 4


def _attn_a_kernel(q_ref, k_ref, vt_ref, mixed_ref, o_ref, acc_scr, q_scr, s0_scr, s1_scr,
                   *, seq, tq, nqb):
    def load_q_rows(qb):
        q = q_ref[qb * tq:(qb + 1) * tq, :]
        return jnp.concatenate(
            [q[:, g * HEAD_DIM:(g + 1) * HEAD_DIM] for g in range(GROUP)], axis=0)

    _flash_sweep(load_q_rows, nqb, k_ref, vt_ref, acc_scr, q_scr, (s0_scr, s1_scr),
                 seq=seq, rows=GROUP * tq)
    for qb in range(nqb):
        out = _flash_result(acc_scr, qb).T
        for g in range(GROUP):
            o_ref[qb * tq:(qb + 1) * tq, g * HEAD_DIM:(g + 1) * HEAD_DIM] = (
                out[g * tq:(g + 1) * tq].astype(o_ref.dtype))


def _attn_a(qkv, vt, mixed, *, seq, nseq, tok0):
    tq = 128
    nqb = _flash_qblocks(seq, tq)
    ts = nqb * tq
    nq = seq // ts
    nc = seq // KEY_CHUNK
    qb0, sb0 = tok0 // ts, tok0 // seq
    rows = GROUP * tq
    gw = GROUP * HEAD_DIM
    return pl.pallas_call(
        functools.partial(_attn_a_kernel, seq=seq, tq=tq, nqb=nqb),
        out_shape=jax.ShapeDtypeStruct(mixed.shape, mixed.dtype),
        grid=(nseq, KV_HEADS, nq),
        in_specs=[
            pl.BlockSpec((ts, gw), lambda b, h, i: (qb0 + b * nq + i, A_Q // GROUP + h)),
            pl.BlockSpec((seq, HEAD_DIM), lambda b, h, i: (sb0 + b, A_K + h)),
            pl.BlockSpec((None, nc, _VT_ROWS, KEY_CHUNK), lambda b, h, i: (h, sb0 + b, 0, 0)),
            pl.BlockSpec(memory_space=pl.ANY),
        ],
        out_specs=pl.BlockSpec((ts, gw), lambda b, h, i: (qb0 + b * nq + i, _MIX_A // gw + h)),
        input_output_aliases={3: 0},
        scratch_shapes=_flash_scratch(nqb, rows),
        compiler_params=_cparams(("parallel", "parallel", "arbitrary"),
                                 _flash_vmem(seq, rows, nqb)),
        name="attn_global_gqa",
    )(qkv, qkv, vt, mixed)


def _attn_d_kernel(q_ref, k_ref, vt_ref, lam_ref, subln_ref, mixed_ref, o_ref, acc_scr, q_scr,
                   s0_scr, s1_scr, *, seq, tq, nqb, lambda_init):
    lane = lax.broadcasted_iota(jnp.int32, (tq, HEAD_DIM), 1)

    def load_q_rows(qb):
        q = q_ref[qb * tq:(qb + 1) * tq, :]
        zero = jnp.zeros_like(q)
        return jnp.concatenate([jnp.where(lane < SUB_DIM, q, zero),
                                jnp.where(lane < SUB_DIM, zero, q)], axis=0)

    _flash_sweep(load_q_rows, nqb, k_ref, vt_ref, acc_scr, q_scr, (s0_scr, s1_scr),
                 seq=seq, rows=2 * tq)
    lp = lam_ref[...]
    lam = (jnp.exp(jnp.sum(lp[0:1] * lp[1:2], axis=1, keepdims=True))
           - jnp.exp(jnp.sum(lp[2:3] * lp[3:4], axis=1, keepdims=True)) + lambda_init)
    for qb in range(nqb):
        out_t = _flash_result(acc_scr, qb)
        d = (out_t[:, :tq] - lam * out_t[:, tq:]).T
        ms = jnp.mean(d * d, axis=-1, keepdims=True)
        y = d * lax.rsqrt(ms + EPS) * subln_ref[...] * (1.0 - lambda_init)
        o_ref[qb * tq:(qb + 1) * tq, :] = y.astype(o_ref.dtype)


def _attn_d(qkv, vt, lam_params, subln, mixed, *, seq, nseq, tok0, lambda_init):
    tq = _pick(seq, (256, 128))
    nqb = _flash_qblocks(seq, tq)
    ts = nqb * tq
    nq = seq // ts
    nc = seq // KEY_CHUNK
    qb0, sb0 = tok0 // ts, tok0 // seq
    rows = 2 * tq
    return pl.pallas_call(
        functools.partial(_attn_d_kernel, seq=seq, tq=tq, nqb=nqb, lambda_init=lambda_init),
        out_shape=jax.ShapeDtypeStruct(mixed.shape, mixed.dtype),
        grid=(nseq, HEADS, nq),
        in_specs=[
            pl.BlockSpec((ts, HEAD_DIM), lambda b, h, i: (qb0 + b * nq + i, D_Q + h)),
            pl.BlockSpec((seq, HEAD_DIM), lambda b, h, i: (sb0 + b, D_K + h)),
            pl.BlockSpec((None, nc, _VT_ROWS, KEY_CHUNK),
                         lambda b, h, i: (KV_HEADS + h, sb0 + b, 0, 0)),
            pl.BlockSpec((4, SUB_DIM), lambda b, h, i: (0, 0)),
            pl.BlockSpec((1, HEAD_DIM), lambda b, h, i: (0, 0)),
            pl.BlockSpec(memory_space=pl.ANY),
        ],
        out_specs=pl.BlockSpec((ts, HEAD_DIM),
                               lambda b, h, i: (qb0 + b * nq + i, _MIX_D // HEAD_DIM + h)),
        input_output_aliases={5: 0},
        scratch_shapes=_flash_scratch(nqb, rows),
        compiler_params=_cparams(("parallel", "parallel", "arbitrary"),
                                 _flash_vmem(seq, rows, nqb)),
        name="attn_differential",
    )(qkv, qkv, vt, lam_params, subln, mixed)


_B_KEYS = 3 * WINDOW
_B_UNROLL = 2


def _window_mask_table():
    o = np.arange(3)[:, None, None] * WINDOW
    r = np.arange(WINDOW)[None, :, None]
    c = np.arange(_B_KEYS)[None, None, :]
    return jnp.asarray(np.where(np.abs(o + r - c) <= WINDOW, 0.0, MASK_VALUE), F32)


def _attn_b_kernel(sink_ref, q_ref, k_ref, v_ref, mask_ref, mixed_ref, o_ref, s0_scr, s1_scr,
                   *, seq, blocks_per_step):
    h = pl.program_id(1)
    step = pl.program_id(2)
    tq = WINDOW
    sink = jnp.concatenate(
        [jnp.full((tq, 1), sink_ref[h * GROUP + g], F32) for g in range(GROUP)], axis=0)

    def band_start(u):
        n = step * blocks_per_step + u
        kstart = jnp.clip((n - 1) * tq, 0, seq - _B_KEYS)
        return pl.multiple_of(kstart, tq), (n * tq - kstart) // tq

    def logits(u, dst):
        kstart, _ = band_start(u)
        q = q_ref[pl.ds(pl.multiple_of(u * tq, tq), tq), :]
        q_rows = jnp.concatenate(
            [q[:, g * HEAD_DIM:(g + 1) * HEAD_DIM] for g in range(GROUP)], axis=0)
        dst[...] = lax.dot_general(q_rows, k_ref[pl.ds(kstart, _B_KEYS), :],
                                   (((1,), (1,)), ((), ())),
                                   preferred_element_type=F32)

    def attend(u, src, carry):
        kstart, offset = band_start(u)
        mask = mask_ref[offset]
        s = src[...] + jnp.concatenate([mask] * GROUP, axis=0)
        m = jnp.maximum(jnp.max(s, axis=1, keepdims=True), sink)
        p = jnp.exp(s - m)
        denom = jnp.sum(p, axis=1, keepdims=True) + jnp.exp(sink - m)
        out = jnp.dot(p.astype(BF16), v_ref[pl.ds(kstart, _B_KEYS), :],
                      preferred_element_type=F32) / denom
        rows = pl.ds(pl.multiple_of(u * tq, tq), tq)
        for g in range(GROUP):
            o_ref[rows, g * HEAD_DIM:(g + 1) * HEAD_DIM] = (
                out[g * tq:(g + 1) * tq].astype(o_ref.dtype))
        return carry

    _pipelined(blocks_per_step, logits, attend, (s0_scr, s1_scr), 0, _B_UNROLL)


def _attn_b(qkv, sink, mixed, *, seq, nseq, tok0):
    nblk = _pick(seq // WINDOW, (16, 8, 4, 2))
    ts = nblk * WINDOW
    assert seq >= _B_KEYS and seq % ts == 0
    nq = seq // ts
    qb0, sb0 = tok0 // ts, tok0 // seq
    gw = GROUP * HEAD_DIM
    rows = GROUP * WINDOW
    vmem = (2 * (2 * seq * HEAD_DIM * 2 + 2 * ts * gw * 2) + 3 * WINDOW * _B_KEYS * 4
            + 10 * rows * _B_KEYS * 4)
    return pl.pallas_call(
        functools.partial(_attn_b_kernel, seq=seq, blocks_per_step=nblk),
        out_shape=jax.ShapeDtypeStruct(mixed.shape, mixed.dtype),
        grid_spec=pltpu.PrefetchScalarGridSpec(
            num_scalar_prefetch=1,
            grid=(nseq, KV_HEADS, nq),
            in_specs=[
                pl.BlockSpec((ts, gw), lambda b, h, i, s_: (qb0 + b * nq + i, B_Q // GROUP + h)),
                pl.BlockSpec((seq, HEAD_DIM), lambda b, h, i, s_: (sb0 + b, B_K + h)),
                pl.BlockSpec((seq, HEAD_DIM), lambda b, h, i, s_: (sb0 + b, B_V + h)),
                pl.BlockSpec((3, WINDOW, _B_KEYS), lambda b, h, i, s_: (0, 0, 0)),
                pl.BlockSpec(memory_space=pl.ANY),
            ],
            out_specs=pl.BlockSpec(
                (ts, gw), lambda b, h, i, s_: (qb0 + b * nq + i, _MIX_B // gw + h)),
            scratch_shapes=_logits_scratch(rows, _B_KEYS),
        ),
        input_output_aliases={5: 0},
        compiler_params=_cparams(("parallel", "parallel", "arbitrary"), vmem),
        name="attn_window_sink",
    )(sink, qkv, qkv, qkv, _window_mask_table(), mixed)


_C_KEYS = NA_KH * GRID_W
_C_BIAS_ROWS = 2 * NA_KH - 2
_C_ROWS_PER_STAGE = 4
_C_UNROLL = 8


def _attn_c_kernel(q_ref, k_ref, v_ref, bias_ref, mixed_ref, o_ref, s0_scr, s1_scr,
                   *, grid_rows, rows_per_step):
    i = pl.program_id(2)
    stage_rows = _C_ROWS_PER_STAGE * GRID_W

    def window(st, t):
        r = i * rows_per_step + st * _C_ROWS_PER_STAGE + t
        rs = jnp.clip(r - NA_KH // 2, 0, grid_rows - NA_KH)
        return pl.multiple_of(rs * GRID_W, GRID_W), rs - r + NA_KH - 1

    def logits(st, dst):
        for t in range(_C_ROWS_PER_STAGE):
            kstart, _ = window(st, t)
            q = q_ref[pl.ds(pl.multiple_of(st * stage_rows + t * GRID_W, GRID_W), GRID_W), :]
            dst[t * GRID_W:(t + 1) * GRID_W, :] = lax.dot_general(
                q, k_ref[pl.ds(kstart, _C_KEYS), :], (((1,), (1,)), ((), ())),
                preferred_element_type=F32)

    def attend(st, src, carry):
        wins = [window(st, t) for t in range(_C_ROWS_PER_STAGE)]
        bias = jnp.concatenate(
            [jnp.concatenate([bias_ref[dr0 + 2 * u] for u in range(NA_KH // 2)], axis=1)
             for _, dr0 in wins], axis=0)
        s = src[...] + bias
        m = jnp.max(s, axis=1, keepdims=True)
        p = jnp.exp(s - m)
        denom = jnp.sum(p, axis=1, keepdims=True)
        pb = p.astype(BF16)
        out = jnp.concatenate(
            [jnp.dot(pb[t * GRID_W:(t + 1) * GRID_W], v_ref[pl.ds(kstart, _C_KEYS), :],
                     preferred_element_type=F32) for t, (kstart, _) in enumerate(wins)], axis=0)
        rows = pl.ds(pl.multiple_of(st * stage_rows, stage_rows), stage_rows)
        o_ref[rows, :] = (out / denom).astype(o_ref.dtype)
        return carry

    _pipelined(rows_per_step // _C_ROWS_PER_STAGE, logits, attend, (s0_scr, s1_scr), 0,
               _C_UNROLL)


def _neighbourhood_bias(rpb):
    c = np.arange(GRID_W)[:, None]
    kc = np.arange(GRID_W)[None, :]
    cs = np.clip(c - NA_KW // 2, 0, GRID_W - NA_KW)
    inside = (kc >= cs) & (kc < cs + NA_KW)
    dc = np.clip(kc - c + NA_KW - 1, 0, 2 * NA_KW - 2)
    select = np.zeros((2 * NA_KW - 1, GRID_W * GRID_W), np.float32)
    select[dc.reshape(-1), np.arange(GRID_W * GRID_W)] = 1.0
    base = jnp.einsum("lhdj,jx->lhdx", rpb.astype(F32), jnp.asarray(select),
                      precision=lax.Precision.HIGHEST)
    base = base.reshape(rpb.shape[:3] + (GRID_W, GRID_W))
    base = jnp.where(inside, base, MASK_VALUE)
    return jnp.concatenate([base[:, :, :-1], base[:, :, 1:]], axis=-1)


def _attn_c(qkv, bias, l, mixed, *, seq, nseq, tok0):
    grid_rows = seq // GRID_W
    assert grid_rows >= NA_KH
    rows_per_step = _pick(grid_rows, (32, 16, 8))
    tq = rows_per_step * GRID_W
    nq = seq // tq
    qb0, sb0 = tok0 // tq, tok0 // seq
    vmem = (2 * (2 * seq * HEAD_DIM * 2 + 2 * tq * HEAD_DIM * 2
                 + _C_BIAS_ROWS * GRID_W * 2 * GRID_W * 4)
            + 12 * _C_ROWS_PER_STAGE * GRID_W * _C_KEYS * 4)
    return pl.pallas_call(
        functools.partial(_attn_c_kernel, grid_rows=grid_rows, rows_per_step=rows_per_step),
        out_shape=jax.ShapeDtypeStruct(mixed.shape, mixed.dtype),
        grid=(nseq, HEADS, nq),
        in_specs=[
            pl.BlockSpec((tq, HEAD_DIM), lambda b, h, i: (qb0 + b * nq + i, C_Q + h)),
            pl.BlockSpec((seq, HEAD_DIM), lambda b, h, i: (sb0 + b, C_K + h)),
            pl.BlockSpec((seq, HEAD_DIM), lambda b, h, i: (sb0 + b, C_V + h)),
            pl.BlockSpec((None, None, _C_BIAS_ROWS, GRID_W, 2 * GRID_W),
                         lambda b, h, i: (l, h, 0, 0, 0)),
            pl.BlockSpec(memory_space=pl.ANY),
        ],
        out_specs=pl.BlockSpec((tq, HEAD_DIM),
                               lambda b, h, i: (qb0 + b * nq + i, _MIX_C // HEAD_DIM + h)),
        input_output_aliases={4: 0},
        scratch_shapes=_logits_scratch(_C_ROWS_PER_STAGE * GRID_W, _C_KEYS),
        compiler_params=_cparams(("parallel", "parallel", "arbitrary"), vmem),
        name="attn_neighbourhood",
    )(qkv, qkv, qkv, bias, mixed)


def _rope_tables(seqs):
    pos = np.concatenate([np.tile(np.arange(s), n) for s, n in seqs]).astype(np.int32)
    pos = jnp.asarray(pos)
    row = (pos // GRID_W).astype(F32)[:, None]
    col = (pos % GRID_W).astype(F32)[:, None]
    t = pos.astype(F32)[:, None]
    inv64 = ROPE_THETA ** (-jnp.arange(0, SUB_DIM, 2, dtype=F32) / SUB_DIM)[None, :]
    inv128 = ROPE_THETA ** (-jnp.arange(0, HEAD_DIM, 2, dtype=F32) / HEAD_DIM)[None, :]

    def pair(ang):
        c, s = jnp.cos(ang), jnp.sin(ang)
        return jnp.concatenate([c, c], axis=1), jnp.concatenate([-s, s], axis=1)

    cr, sr = pair(row * inv64)
    cc, sc = pair(col * inv64)
    cb, sb = pair(t * inv128)
    cd, sd = pair(t * inv64)
    return jnp.concatenate([cr, cc, sr, sc, cb, sb, cd, cd, sd, sd], axis=1)


def _qkv_gains(p, l):
    scale = HEAD_DIM ** -0.5
    scale_d = SUB_DIM ** -0.5
    log2e = math.log2(math.e)
    ones = lambda n: jnp.ones((n * HEAD_DIM,), F32)
    tile = lambda g, n: jnp.tile(g.astype(F32), n)
    parts = [
        tile(p["q_norm_a"][l], HEADS) * (scale * log2e), tile(p["k_norm_a"][l], KV_HEADS),
        ones(KV_HEADS),
        tile(p["q_norm_b"][l], HEADS) * scale, tile(p["k_norm_b"][l], KV_HEADS), ones(KV_HEADS),
        tile(p["q_norm_c"][l], HEADS) * scale, tile(p["k_norm_c"][l], HEADS), ones(HEADS),
        tile(p["q_norm_d"][l], 2 * HEADS) * (scale_d * log2e), tile(p["k_norm_d"][l], 2 * HEADS),
        ones(HEADS),
    ]
    return jnp.concatenate(parts).reshape(1, QKV_BLOCKS * HEAD_DIM)


def _ffn(x, g, wg, wu, wd, l):
    xn = _rmsnorm(x, g)
    hidden = _gateup(xn, wg, wu, l)
    tk_candidates = tuple(m * MXU_DIM for m in (11, 8, 4, 2, 1))
    return _mm_res(hidden, wd, l, x, FFN_RESIDUAL, tn_candidates=(1024, 512, 256, 128),
                   tk_candidates=tk_candidates)


def _token_mixing(x, p, l, seqs, rope, bias, w_in, w_out):
    lambda_init = 0.8 - 0.6 * math.exp(-0.3 * l)
    xn = _rmsnorm(x, p["norm_mix"][l])
    proj = _mm(xn, w_in, l)
    qkv = _prep(proj, _qkv_gains(p, l), rope)
    vt = _values_transposed(proj)
    sink = p["sink_b"][l].astype(F32)
    lam_params = jnp.stack([p["lambda_q1"][l], p["lambda_k1"][l],
                            p["lambda_q2"][l], p["lambda_k2"][l]]).astype(F32)
    subln = p["subln_d"][l].astype(F32).reshape(1, HEAD_DIM)
    mixed = jnp.zeros((x.shape[0], 4 * MIX_Q), BF16)
    tok0 = 0
    for seq, nseq in seqs:
        kw = dict(seq=seq, nseq=nseq, tok0=tok0)
        mixed = _attn_a(qkv, vt, mixed, **kw)
        mixed = _attn_b(qkv, sink, mixed, **kw)
        mixed = _attn_c(qkv, bias, l, mixed, **kw)
        mixed = _attn_d(qkv, vt, lam_params, subln, mixed, lambda_init=lambda_init, **kw)
        tok0 += seq * nseq
    return _mm_res(mixed, w_out, l, x, 1.0, tn_candidates=(512, 256, 128),
                   tk_candidates=(mixed.shape[1],))


def kernel(x_prompt, x_sample, norm_ffn1, w_ffn1_gate, w_ffn1_up, w_ffn1_down, norm_mix, w_in,
           q_norm_a, k_norm_a, q_norm_b, k_norm_b, sink_b, q_norm_c, k_norm_c, rpb_c,
           q_norm_d, k_norm_d, lambda_q1, lambda_k1, lambda_q2, lambda_k2, subln_d, w_out,
           norm_ffn2, w_ffn2_gate, w_ffn2_up, w_ffn2_down):
    p = dict(norm_mix=norm_mix, q_norm_a=q_norm_a, k_norm_a=k_norm_a, q_norm_b=q_norm_b,
             k_norm_b=k_norm_b, sink_b=sink_b, q_norm_c=q_norm_c, k_norm_c=k_norm_c, rpb_c=rpb_c,
             q_norm_d=q_norm_d, k_norm_d=k_norm_d, lambda_q1=lambda_q1, lambda_k1=lambda_k1,
             lambda_q2=lambda_q2, lambda_k2=lambda_k2, subln_d=subln_d)
    depth = w_in.shape[0]
    d_model = x_prompt.shape[-1]
    bp, sp, _ = x_prompt.shape
    bs, ss, _ = x_sample.shape
    seqs = ((sp, bp), (ss, bs))
    assert (bp * sp) % ss == 0
    x = jnp.concatenate([x_prompt.reshape(bp * sp, d_model),
                         x_sample.reshape(bs * ss, d_model)], axis=0)
    rope = _rope_tables(seqs)
    bias = _neighbourhood_bias(rpb_c)
    ffn1 = (w_ffn1_gate, w_ffn1_up, w_ffn1_down.astype(BF16))
    ffn2 = (w_ffn2_gate, w_ffn2_up, w_ffn2_down.astype(BF16))
    for l in range(depth):
        x = _ffn(x, norm_ffn1[l], *ffn1, l)
        x = _token_mixing(x, p, l, seqs, rope, bias, w_in, w_out)
        x = _ffn(x, norm_ffn2[l], *ffn2, l)
    y_prompt = x[:bp * sp].reshape(bp, sp, d_model)
    y_sample = x[bp * sp:].reshape(bs, ss, d_model)
    return (y_prompt, y_sample)
```

```python
import functools
import math

import numpy as np
import jax
import jax.numpy as jnp
from jax import lax
from jax.experimental import pallas as pl
from jax.experimental.pallas import tpu as pltpu

F32 = jnp.float32
BF16 = jnp.bfloat16

HEAD_DIM = 128
SUB_DIM = HEAD_DIM // 2
HEADS = 8
KV_HEADS = 2
GROUP = HEADS // KV_HEADS
GRID_W = 64
NA_KH = 8
NA_KW = 16
WINDOW = 128
ROPE_THETA = 10000.0
EPS = 1e-6
FFN_RESIDUAL = 0.5
MASK_VALUE = -1e30

QKV_BLOCKS = 72
A_Q, A_K, A_V = 0, 8, 10
B_Q, B_K, B_V = 12, 20, 22
C_Q, C_K, C_V = 24, 32, 40
D_Q, D_K, D_V = 48, 56, 64
MIX_Q = HEADS * HEAD_DIM
_MIX_A, _MIX_B, _MIX_C, _MIX_D = 0, MIX_Q, 2 * MIX_Q, 3 * MIX_Q

VMEM_CAP_BYTES = 60 * 1024 * 1024
MIB = 1024 * 1024
MXU_DIM = 256

FFN_TN = 512
FFN_PAD = 512
KEY_CHUNK = 512
FLASH_ITEMS_PER_STEP = 128
FLASH_UNROLL = 16


def _cparams(semantics, vmem_bytes):
    return pltpu.CompilerParams(
        dimension_semantics=semantics,
        vmem_limit_bytes=int(min(VMEM_CAP_BYTES, max(vmem_bytes, 16 * MIB))))


def _pick(n, candidates):
    for c in candidates:
        if n % c == 0:
            return c
    return n


def _rmsnorm_kernel(x_ref, g_ref, o_ref):
    x = x_ref[...]
    ms = jnp.mean(x * x, axis=-1, keepdims=True)
    o_ref[...] = (x * lax.rsqrt(ms + EPS) * g_ref[...]).astype(o_ref.dtype)


def _rmsnorm(x, g):
    t, d = x.shape
    tm = _pick(t, (512, 256, 128, 64, 32, 16, 8))
    return pl.pallas_call(
        _rmsnorm_kernel,
        out_shape=jax.ShapeDtypeStruct((t, d), BF16),
        grid=(t // tm,),
        in_specs=[pl.BlockSpec((tm, d), lambda i: (i, 0)),
                  pl.BlockSpec((1, d), lambda i: (0, 0))],
        out_specs=pl.BlockSpec((tm, d), lambda i: (i, 0)),
        compiler_params=_cparams(("parallel",), 2 * tm * d * 6 + 4 * tm * d * 4),
        name="rmsnorm",
    )(x, g.reshape(1, d).astype(F32))


def _gateup_kernel(x_ref, wg_ref, wu_ref, o_ref, *, d_ff, padded):
    x = x_ref[...]
    g = jnp.dot(x, wg_ref[...].astype(BF16), preferred_element_type=F32)
    u = jnp.dot(x, wu_ref[...].astype(BF16), preferred_element_type=F32)
    h = g * (1.0 / (1.0 + jnp.exp(-g))) * u
    if padded:
        tn = o_ref.shape[1]
        col = pl.program_id(1) * tn + lax.broadcasted_iota(jnp.int32, h.shape, 1)
        h = jnp.where(col < d_ff, h, 0.0)
    o_ref[...] = h.astype(o_ref.dtype)


def _gateup(xn, wg, wu, l):
    t, d = xn.shape
    d_ff = wg.shape[2]
    tm = _pick(t, (1024, 512, 256, 128))
    tn = FFN_TN
    width = -(-d_ff // FFN_PAD) * FFN_PAD
    last = pl.cdiv(d_ff, tn) - 1
    wbytes = wg.dtype.itemsize
    vmem = (tm * d * 2 + 2 * (2 * d * tn * wbytes + tm * tn * 2) + 2 * d * tn * 2
            + 6 * tm * tn * 4)
    w_spec = pl.BlockSpec((None, d, tn), lambda i, j: (l, 0, jnp.minimum(j, last)))
    x_spec = pl.BlockSpec((tm, d), lambda i, j: (i, 0), pipeline_mode=pl.Buffered(1))
    return pl.pallas_call(
        functools.partial(_gateup_kernel, d_ff=d_ff, padded=width != d_ff),
        out_shape=jax.ShapeDtypeStruct((t, width), BF16),
        grid=(t // tm, width // tn),
        in_specs=[x_spec, w_spec, w_spec],
        out_specs=pl.BlockSpec((tm, tn), lambda i, j: (i, j)),
        compiler_params=_cparams(("parallel", "arbitrary"), vmem),
        name="ffn_gateup",
    )(xn, wg, wu)


def _mm_res_kernel(a_ref, w_ref, r_ref, o_ref, *scratch, nk, scale, k_valid):
    tk = w_ref.shape[0]

    def partial_sum(mask_tail):
        w = w_ref[...].astype(BF16)
        if mask_tail:
            row = (nk - 1) * tk + lax.broadcasted_iota(jnp.int32, w.shape, 0)
            w = jnp.where(row < k_valid, w, jnp.zeros_like(w))
        return jnp.dot(a_ref[...], w, preferred_element_type=F32)

    ragged = k_valid < nk * tk
    if nk == 1:
        o_ref[...] = r_ref[...] + scale * partial_sum(ragged)
        return
    acc_ref, = scratch
    k = pl.program_id(2)

    @pl.when(k == 0)
    def _():
        acc_ref[...] = partial_sum(False)

    @pl.when(jnp.logical_and(k > 0, k < nk - 1))
    def _():
        acc_ref[...] += partial_sum(False)

    @pl.when(k == nk - 1)
    def _():
        o_ref[...] = r_ref[...] + scale * (acc_ref[...] + partial_sum(ragged))


def _mm_res(a, w, l, res, scale, *, tn_candidates, tk_candidates):
    t, ka = a.shape
    k_valid, n = w.shape[1], w.shape[2]
    tm = _pick(t, (1024, 512, 256, 128))
    tn = _pick(n, tn_candidates)
    tk = _pick(ka, tk_candidates)
    nk = ka // tk
    assert (nk - 1) * tk < k_valid <= ka
    scratch = [pltpu.VMEM((tm, tn), F32)] if nk > 1 else []
    vmem = (2 * (tm * tk * 2 + tk * tn * w.dtype.itemsize + 2 * tm * tn * 4) + tk * tn * 2
            + 3 * tm * tn * 4)
    return pl.pallas_call(
        functools.partial(_mm_res_kernel, nk=nk, scale=scale, k_valid=k_valid),
        out_shape=jax.ShapeDtypeStruct((t, n), F32),
        grid=(t // tm, n // tn, nk),
        in_specs=[pl.BlockSpec((tm, tk), lambda i, j, k: (i, k)),
                  pl.BlockSpec((None, tk, tn), lambda i, j, k: (l, k, j)),
                  pl.BlockSpec((tm, tn), lambda i, j, k: (i, j))],
        out_specs=pl.BlockSpec((tm, tn), lambda i, j, k: (i, j)),
        scratch_shapes=scratch,
        compiler_params=_cparams(("parallel", "parallel", "arbitrary"), vmem),
        name="matmul_residual",
    )(a, w, res)


def _mm_kernel(a_ref, w_ref, o_ref):
    o_ref[...] = jnp.dot(a_ref[...], w_ref[...].astype(BF16), preferred_element_type=F32)


def _mm(a, w, l):
    t, kdim = a.shape
    n = w.shape[2]
    tm = _pick(t, (1024, 512, 256, 128))
    tn = _pick(n, (512, 256, 128))
    vmem = (2 * (tm * kdim * 2 + kdim * tn * w.dtype.itemsize + tm * tn * 4) + kdim * tn * 2
            + 2 * tm * tn * 4)
    return pl.pallas_call(
        _mm_kernel,
        out_shape=jax.ShapeDtypeStruct((t, n), F32),
        grid=(t // tm, n // tn),
        in_specs=[pl.BlockSpec((tm, kdim), lambda i, j: (i, 0)),
                  pl.BlockSpec((None, kdim, tn), lambda i, j: (l, 0, j))],
        out_specs=pl.BlockSpec((tm, tn), lambda i, j: (i, j)),
        compiler_params=_cparams(("parallel", "arbitrary"), vmem),
        name="in_proj",
    )(a, w)


_PREP_CHUNK_KINDS = (
    "a" * (HEADS + KV_HEADS) + "v" * KV_HEADS + "b" * (HEADS + KV_HEADS) + "v" * KV_HEADS
    + "c" * (2 * HEADS) + "v" * HEADS + "d" * (2 * HEADS) + "v" * HEADS)
_PREP_CHUNKS = 8
_PREP_COLS = _PREP_CHUNKS * HEAD_DIM
_PREP_BLOCK_KINDS = tuple(
    tuple(_PREP_CHUNK_KINDS[i:i + _PREP_CHUNKS]) for i in range(0, QKV_BLOCKS, _PREP_CHUNKS))
_ROPE_OFF = {"a": 0, "b": 2 * HEAD_DIM, "d": 4 * HEAD_DIM}


def _prep_kernel(p_ref, g_ref, rope_ref, perm_ref, o_ref):
    j = pl.program_id(1)
    tm = p_ref.shape[0]
    lane = lax.broadcasted_iota(jnp.int32, (tm, HEAD_DIM), 1)
    low_half = lane < SUB_DIM

    def chunk(c, kind):
        cols = slice(c * HEAD_DIM, (c + 1) * HEAD_DIM)
        x = p_ref[:, cols]
        if kind == "v":
            o_ref[:, cols] = x.astype(o_ref.dtype)
            return
        sq = x * x
        if kind == "d":
            lo = jnp.sum(jnp.where(low_half, sq, 0.0), axis=-1, keepdims=True)
            hi = jnp.sum(jnp.where(low_half, 0.0, sq), axis=-1, keepdims=True)
            ms = jnp.where(low_half, lo, hi) * (1.0 / SUB_DIM)
        else:
            ms = jnp.mean(sq, axis=-1, keepdims=True)
        y = x * lax.rsqrt(ms + EPS) * g_ref[:, cols]
        if kind != "c":
            off = _ROPE_OFF[kind]
            cos = rope_ref[:, off:off + HEAD_DIM]
            sin = rope_ref[:, off + HEAD_DIM:off + 2 * HEAD_DIM]
            perm = perm_ref[0 if kind == "b" else 1]
            y_hi = y.astype(BF16)
            y_lo = (y - y_hi.astype(F32)).astype(BF16)
            partner = (jnp.dot(y_hi, perm, preferred_element_type=F32)
                       + jnp.dot(y_lo, perm, preferred_element_type=F32))
            y = y * cos + partner * sin
        o_ref[:, cols] = y.astype(o_ref.dtype)

    groups = {}
    for jj, kinds in enumerate(_PREP_BLOCK_KINDS):
        groups.setdefault(kinds, []).append(jj)
    for kinds, js in groups.items():
        cond = functools.reduce(jnp.logical_or, [j == jj for jj in js])

        @pl.when(cond)
        def _(kinds=kinds):
            for c, kind in enumerate(kinds):
                chunk(c, kind)


def _rope_permutations():
    i = np.arange(HEAD_DIM)
    full = (i + SUB_DIM) % HEAD_DIM
    half = np.where((i % SUB_DIM) < SUB_DIM // 2, i + SUB_DIM // 2, i - SUB_DIM // 2)
    mats = np.zeros((2, HEAD_DIM, HEAD_DIM), np.float32)
    mats[0, full, i] = 1.0
    mats[1, half, i] = 1.0
    return jnp.asarray(mats, BF16)


def _prep(proj, gains, rope):
    t, w = proj.shape
    assert w == QKV_BLOCKS * HEAD_DIM
    tm = _pick(t, (1024, 512, 256, 128, 64, 32, 16))
    rw = rope.shape[1]
    vmem = 2 * (tm * _PREP_COLS * 6 + tm * rw * 4) + 8 * tm * _PREP_COLS * 4
    return pl.pallas_call(
        _prep_kernel,
        out_shape=jax.ShapeDtypeStruct((t, w), BF16),
        grid=(t // tm, w // _PREP_COLS),
        in_specs=[pl.BlockSpec((tm, _PREP_COLS), lambda i, j: (i, j)),
                  pl.BlockSpec((1, _PREP_COLS), lambda i, j: (0, j)),
                  pl.BlockSpec((tm, rw), lambda i, j: (i, 0)),
                  pl.BlockSpec((2, HEAD_DIM, HEAD_DIM), lambda i, j: (0, 0, 0))],
        out_specs=pl.BlockSpec((tm, _PREP_COLS), lambda i, j: (i, j)),
        compiler_params=_cparams(("parallel", "arbitrary"), vmem),
        name="qkv_norm_rope",
    )(proj, gains, rope, _rope_permutations())


_VT_HEADS = KV_HEADS + HEADS
_BF16_SUBLANES = 16
_VT_ROWS = HEAD_DIM + _BF16_SUBLANES


def _vt_kernel(va_ref, vd_ref, o_ref):
    ones = jnp.ones((_BF16_SUBLANES, o_ref.shape[-1]), o_ref.dtype)
    heads = ([(va_ref, h) for h in range(KV_HEADS)] + [(vd_ref, h) for h in range(HEADS)])
    for slot, (ref, h) in enumerate(heads):
        o_ref[slot, :HEAD_DIM, :] = ref[:, h * HEAD_DIM:(h + 1) * HEAD_DIM].T.astype(o_ref.dtype)
        o_ref[slot, HEAD_DIM:, :] = ones


def _values_transposed(proj):
    t = proj.shape[0]
    tk = KEY_CHUNK
    wa, wd = KV_HEADS * HEAD_DIM, HEADS * HEAD_DIM
    return pl.pallas_call(
        _vt_kernel,
        out_shape=jax.ShapeDtypeStruct((_VT_HEADS, t // tk, _VT_ROWS, tk), BF16),
        grid=(t // tk,),
        in_specs=[pl.BlockSpec((tk, wa), lambda i: (i, A_V * HEAD_DIM // wa)),
                  pl.BlockSpec((tk, wd), lambda i: (i, D_V * HEAD_DIM // wd))],
        out_specs=pl.BlockSpec((_VT_HEADS, None, _VT_ROWS, tk), lambda i: (0, i, 0, 0)),
        compiler_params=_cparams(("parallel",), 8 * tk * (wa + wd) * 4),
        name="values_transposed",
    )(proj, proj)


def _pipelined(n, produce, consume, bufs, carry, unroll):
    if n == 1:
        produce(0, bufs[0])
        return consume(0, bufs[0], carry)
    assert n % 2 == 0
    while n % unroll:
        unroll //= 2

    def group(first, carry, is_last):
        for u in range(0, unroll, 2):
            produce(first + u + 1, bufs[1])
            carry = consume(first + u, bufs[0], carry)
            if not (is_last and u == unroll - 2):
                produce(first + u + 2, bufs[0])
            carry = consume(first + u + 1, bufs[1], carry)
        return carry

    produce(0, bufs[0])
    carry = lax.fori_loop(0, n // unroll - 1,
                          lambda c, carry: group(c * unroll, carry, False), carry)
    return group(n - unroll, carry, True)


def _logits_scratch(rows, cols):
    return [pltpu.VMEM((rows, cols), F32), pltpu.VMEM((rows, cols), F32)]


def _flash_sweep(load_q_rows, n_qblocks, k_ref, vt_ref, acc_scr, q_scr, s_bufs, *, seq, rows):
    tk = KEY_CHUNK
    n = seq // tk
    acc_scr[...] = jnp.zeros(acc_scr.shape, F32)
    for qb in range(n_qblocks):
        q_scr[qb] = load_q_rows(qb).T

    def logits(t, dst):
        qb, c = lax.div(t, n), lax.rem(t, n)
        k_c = k_ref[pl.ds(pl.multiple_of(c * tk, tk), tk), :]
        dst[...] = jnp.dot(k_c, q_scr[qb], preferred_element_type=F32)

    def update(t, src, m):
        qb, c = lax.div(t, n), lax.rem(t, n)
        s = src[...]
        m = jnp.where(c == 0, -jnp.inf, m)
        m_new = jnp.maximum(m, jnp.max(s, axis=0, keepdims=True))
        alpha = jnp.exp2(m - m_new)
        p = jnp.exp2(s - m_new).astype(BF16)
        pv = jnp.dot(vt_ref[c], p, preferred_element_type=F32)
        acc_scr[qb] = alpha * acc_scr[qb] + pv
        return m_new

    _pipelined(n_qblocks * n, logits, update, s_bufs, jnp.full((1, rows), -jnp.inf, F32),
               FLASH_UNROLL)


def _flash_result(acc_scr, qb):
    acc = acc_scr[qb]
    return acc[:HEAD_DIM, :] * (1.0 / acc[HEAD_DIM:HEAD_DIM + 1, :])


def _flash_qblocks(seq, tq):
    want = max(1, FLASH_ITEMS_PER_STEP // (seq // KEY_CHUNK))
    return _pick(seq // tq, tuple(c for c in (8, 4, 2, 1) if c <= want))


def _flash_scratch(n_qblocks, rows):
    return ([pltpu.VMEM((n_qblocks, _VT_ROWS, rows), F32),
             pltpu.VMEM((n_qblocks, HEAD_DIM, rows), BF16)] + _logits_scratch(KEY_CHUNK, rows))


def _flash_vmem(seq, rows, n_qblocks):
    blocks = 2 * (seq * HEAD_DIM * 2 + seq * _VT_ROWS * 2 + 2 * n_qblocks * rows * HEAD_DIM * 2)
    return blocks + n_qblocks * rows * _VT_ROWS * 4 + 12 * rows * KEY_CHUNK * 4


def _attn_a_kernel(q_ref, k_ref, vt_ref, mixed_ref, o_ref, acc_scr, q_scr, s0_scr, s1_scr,
                   *, seq, tq, nqb):
    def load_q_rows(qb):
        q = q_ref[qb * tq:(qb + 1) * tq, :]
        return jnp.concatenate(
            [q[:, g * HEAD_DIM:(g + 1) * HEAD_DIM] for g in range(GROUP)], axis=0)

    _flash_sweep(load_q_rows, nqb, k_ref, vt_ref, acc_scr, q_scr, (s0_scr, s1_scr),
                 seq=seq, rows=GROUP * tq)
    for qb in range(nqb):
        out = _flash_result(acc_scr, qb).T
        for g in range(GROUP):
            o_ref[qb * tq:(qb + 1) * tq, g * HEAD_DIM:(g + 1) * HEAD_DIM] = (
                out[g * tq:(g + 1) * tq].astype(o_ref.dtype))


def _attn_a(qkv, vt, mixed, *, seq, nseq, tok0):
    tq = 128
    nqb = _flash_qblocks(seq, tq)
    ts = nqb * tq
    nq = seq // ts
    nc = seq // KEY_CHUNK
    qb0, sb0 = tok0 // ts, tok0 // seq
    rows = GROUP * tq
    gw = GROUP * HEAD_DIM
    return pl.pallas_call(
        functools.partial(_attn_a_kernel, seq=seq, tq=tq, nqb=nqb),
        out_shape=jax.ShapeDtypeStruct(mixed.shape, mixed.dtype),
        grid=(nseq, KV_HEADS, nq),
        in_specs=[
            pl.BlockSpec((ts, gw), lambda b, h, i: (qb0 + b * nq + i, A_Q // GROUP + h)),
            pl.BlockSpec((seq, HEAD_DIM), lambda b, h, i: (sb0 + b, A_K + h)),
            pl.BlockSpec((None, nc, _VT_ROWS, KEY_CHUNK), lambda b, h, i: (h, sb0 + b, 0, 0)),
            pl.BlockSpec(memory_space=pl.ANY),
        ],
        out_specs=pl.BlockSpec((ts, gw), lambda b, h, i: (qb0 + b * nq + i, _MIX_A // gw + h)),
        input_output_aliases={3: 0},
        scratch_shapes=_flash_scratch(nqb, rows),
        compiler_params=_cparams(("parallel", "parallel", "arbitrary"),
                                 _flash_vmem(seq, rows, nqb)),
        name="attn_global_gqa",
    )(qkv, qkv, vt, mixed)


def _attn_d_kernel(q_ref, k_ref, vt_ref, lam_ref, subln_ref, mixed_ref, o_ref, acc_scr, q_scr,
                   s0_scr, s1_scr, *, seq, tq, nqb, lambda_init):
    lane = lax.broadcasted_iota(jnp.int32, (tq, HEAD_DIM), 1)

    def load_q_rows(qb):
        q = q_ref[qb * tq:(qb + 1) * tq, :]
        zero = jnp.zeros_like(q)
        return jnp.concatenate([jnp.where(lane < SUB_DIM, q, zero),
                                jnp.where(lane < SUB_DIM, zero, q)], axis=0)

    _flash_sweep(load_q_rows, nqb, k_ref, vt_ref, acc_scr, q_scr, (s0_scr, s1_scr),
                 seq=seq, rows=2 * tq)
    lp = lam_ref[...]
    lam = (jnp.exp(jnp.sum(lp[0:1] * lp[1:2], axis=1, keepdims=True))
           - jnp.exp(jnp.sum(lp[2:3] * lp[3:4], axis=1, keepdims=True)) + lambda_init)
    for qb in range(nqb):
        out_t = _flash_result(acc_scr, qb)
        d = (out_t[:, :tq] - lam * out_t[:, tq:]).T
        ms = jnp.mean(d * d, axis=-1, keepdims=True)
        y = d * lax.rsqrt(ms + EPS) * subln_ref[...] * (1.0 - lambda_init)
        o_ref[qb * tq:(qb + 1) * tq, :] = y.astype(o_ref.dtype)


def _attn_d(qkv, vt, lam_params, subln, mixed, *, seq, nseq, tok0, lambda_init):
    tq = _pick(seq, (256, 128))
    nqb = _flash_qblocks(seq, tq)
    ts = nqb * tq
    nq = seq // ts
    nc = seq // KEY_CHUNK
    qb0, sb0 = tok0 // ts, tok0 // seq
    rows = 2 * tq
    return pl.pallas_call(
        functools.partial(_attn_d_kernel, seq=seq, tq=tq, nqb=nqb, lambda_init=lambda_init),
        out_shape=jax.ShapeDtypeStruct(mixed.shape, mixed.dtype),
        grid=(nseq, HEADS, nq),
        in_specs=[
            pl.BlockSpec((ts, HEAD_DIM), lambda b, h, i: (qb0 + b * nq + i, D_Q + h)),
            pl.BlockSpec((seq, HEAD_DIM), lambda b, h, i: (sb0 + b, D_K + h)),
            pl.BlockSpec((None, nc, _VT_ROWS, KEY_CHUNK),
                         lambda b, h, i: (KV_HEADS + h, sb0 + b, 0, 0)),
            pl.BlockSpec((4, SUB_DIM), lambda b, h, i: (0, 0)),
            pl.BlockSpec((1, HEAD_DIM), lambda b, h, i: (0, 0)),
            pl.BlockSpec(memory_space=pl.ANY),
        ],
        out_specs=pl.BlockSpec((ts, HEAD_DIM),
                               lambda b, h, i: (qb0 + b * nq + i, _MIX_D // HEAD_DIM + h)),
        input_output_aliases={5: 0},
        scratch_shapes=_flash_scratch(nqb, rows),
        compiler_params=_cparams(("parallel", "parallel", "arbitrary"),
                                 _flash_vmem(seq, rows, nqb)),
        name="attn_differential",
    )(qkv, qkv, vt, lam_params, subln, mixed)


_B_KEYS = 3 * WINDOW
_B_UNROLL = 2


def _window_mask_table():
    o = np.arange(3)[:, None, None] * WINDOW
    r = np.arange(WINDOW)[None, :, None]
    c = np.arange(_B_KEYS)[None, None, :]
    return jnp.asarray(np.where(np.abs(o + r - c) <= WINDOW, 0.0, MASK_VALUE), F32)


def _attn_b_kernel(sink_ref, q_ref, k_ref, v_ref, mask_ref, mixed_ref, o_ref, s0_scr, s1_scr,
                   *, seq, blocks_per_step):
    h = pl.program_id(1)
    step = pl.program_id(2)
    tq = WINDOW
    sink = jnp.concatenate(
        [jnp.full((tq, 1), sink_ref[h * GROUP + g], F32) for g in range(GROUP)], axis=0)

    def band_start(u):
        n = step * blocks_per_step + u
        kstart = jnp.clip((n - 1) * tq, 0, seq - _B_KEYS)
        return pl.multiple_of(kstart, tq), (n * tq - kstart) // tq

    def logits(u, dst):
        kstart, _ = band_start(u)
        q = q_ref[pl.ds(pl.multiple_of(u * tq, tq), tq), :]
        q_rows = jnp.concatenate(
            [q[:, g * HEAD_DIM:(g + 1) * HEAD_DIM] for g in range(GROUP)], axis=0)
        dst[...] = lax.dot_general(q_rows, k_ref[pl.ds(kstart, _B_KEYS), :],
                                   (((1,), (1,)), ((), ())),
                                   preferred_element_type=F32)

    def attend(u, src, carry):
        kstart, offset = band_start(u)
        mask = mask_ref[offset]
        s = src[...] + jnp.concatenate([mask] * GROUP, axis=0)
        m = jnp.maximum(jnp.max(s, axis=1, keepdims=True), sink)
        p = jnp.exp(s - m)
        denom = jnp.sum(p, axis=1, keepdims=True) + jnp.exp(sink - m)
        out = jnp.dot(p.astype(BF16), v_ref[pl.ds(kstart, _B_KEYS), :],
                      preferred_element_type=F32) / denom
        rows = pl.ds(pl.multiple_of(u * tq, tq), tq)
        for g in range(GROUP):
            o_ref[rows, g * HEAD_DIM:(g + 1) * HEAD_DIM] = (
                out[g * tq:(g + 1) * tq].astype(o_ref.dtype))
        return carry

    _pipelined(blocks_per_step, logits, attend, (s0_scr, s1_scr), 0, _B_UNROLL)


def _attn_b(qkv, sink, mixed, *, seq, nseq, tok0):
    nblk = _pick(seq // WINDOW, (16, 8, 4, 2))
    ts = nblk * WINDOW
    assert seq >= _B_KEYS and seq % ts == 0
    nq = seq // ts
    qb0, sb0 = tok0 // ts, tok0 // seq
    gw = GROUP * HEAD_DIM
    rows = GROUP * WINDOW
    vmem = (2 * (2 * seq * HEAD_DIM * 2 + 2 * ts * gw * 2) + 3 * WINDOW * _B_KEYS * 4
            + 10 * rows * _B_KEYS * 4)
    return pl.pallas_call(
        functools.partial(_attn_b_kernel, seq=seq, blocks_per_step=nblk),
        out_shape=jax.ShapeDtypeStruct(mixed.shape, mixed.dtype),
        grid_spec=pltpu.PrefetchScalarGridSpec(
            num_scalar_prefetch=1,
            grid=(nseq, KV_HEADS, nq),
            in_specs=[
                pl.BlockSpec((ts, gw), lambda b, h, i, s_: (qb0 + b * nq + i, B_Q // GROUP + h)),
                pl.BlockSpec((seq, HEAD_DIM), lambda b, h, i, s_: (sb0 + b, B_K + h)),
                pl.BlockSpec((seq, HEAD_DIM), lambda b, h, i, s_: (sb0 + b, B_V + h)),
                pl.BlockSpec((3, WINDOW, _B_KEYS), lambda b, h, i, s_: (0, 0, 0)),
                pl.BlockSpec(memory_space=pl.ANY),
            ],
            out_specs=pl.BlockSpec(
                (ts, gw), lambda b, h, i, s_: (qb0 + b * nq + i, _MIX_B // gw + h)),
            scratch_shapes=_logits_scratch(rows, _B_KEYS),
        ),
        input_output_aliases={5: 0},
        compiler_params=_cparams(("parallel", "parallel", "arbitrary"), vmem),
        name="attn_window_sink",
    )(sink, qkv, qkv, qkv, _window_mask_table(), mixed)


_C_KEYS = NA_KH * GRID_W
_C_BIAS_ROWS = 2 * NA_KH - 2
_C_ROWS_PER_STAGE = 4
_C_UNROLL = 8


def _attn_c_kernel(q_ref, k_ref, v_ref, bias_ref, mixed_ref, o_ref, s0_scr, s1_scr,
                   *, grid_rows, rows_per_step):
    i = pl.program_id(2)
    stage_rows = _C_ROWS_PER_STAGE * GRID_W

    def window(st, t):
        r = i * rows_per_step + st * _C_ROWS_PER_STAGE + t
        rs = jnp.clip(r - NA_KH // 2, 0, grid_rows - NA_KH)
        return pl.multiple_of(rs * GRID_W, GRID_W), rs - r + NA_KH - 1

    def logits(st, dst):
        for t in range(_C_ROWS_PER_STAGE):
            kstart, _ = window(st, t)
            q = q_ref[pl.ds(pl.multiple_of(st * stage_rows + t * GRID_W, GRID_W), GRID_W), :]
            dst[t * GRID_W:(t + 1) * GRID_W, :] = lax.dot_general(
                q, k_ref[pl.ds(kstart, _C_KEYS), :], (((1,), (1,)), ((), ())),
                preferred_element_type=F32)

    def attend(st, src, carry):
        wins = [window(st, t) for t in range(_C_ROWS_PER_STAGE)]
        bias = jnp.concatenate(
            [jnp.concatenate([bias_ref[dr0 + 2 * u] for u in range(NA_KH // 2)], axis=1)
             for _, dr0 in wins], axis=0)
        s = src[...] + bias
        m = jnp.max(s, axis=1, keepdims=True)
        p = jnp.exp(s - m)
        denom = jnp.sum(p, axis=1, keepdims=True)
        pb = p.astype(BF16)
        out = jnp.concatenate(
            [jnp.dot(pb[t * GRID_W:(t + 1) * GRID_W], v_ref[pl.ds(kstart, _C_KEYS), :],
                     preferred_element_type=F32) for t, (kstart, _) in enumerate(wins)], axis=0)
        rows = pl.ds(pl.multiple_of(st * stage_rows, stage_rows), stage_rows)
        o_ref[rows, :] = (out / denom).astype(o_ref.dtype)
        return carry

    _pipelined(rows_per_step // _C_ROWS_PER_STAGE, logits, attend, (s0_scr, s1_scr), 0,
               _C_UNROLL)


def _neighbourhood_bias(rpb):
    c = np.arange(GRID_W)[:, None]
    kc = np.arange(GRID_W)[None, :]
    cs = np.clip(c - NA_KW // 2, 0, GRID_W - NA_KW)
    inside = (kc >= cs) & (kc < cs + NA_KW)
    dc = np.clip(kc - c + NA_KW - 1, 0, 2 * NA_KW - 2)
    select = np.zeros((2 * NA_KW - 1, GRID_W * GRID_W), np.float32)
    select[dc.reshape(-1), np.arange(GRID_W * GRID_W)] = 1.0
    base = jnp.einsum("lhdj,jx->lhdx", rpb.astype(F32), jnp.asarray(select),
                      precision=lax.Precision.HIGHEST)
    base = base.reshape(rpb.shape[:3] + (GRID_W, GRID_W))
    base = jnp.where(inside, base, MASK_VALUE)
    return jnp.concatenate([base[:, :, :-1], base[:, :, 1:]], axis=-1)


def _attn_c(qkv, bias, l, mixed, *, seq, nseq, tok0):
    grid_rows = seq // GRID_W
    assert grid_rows >= NA_KH
    rows_per_step = _pick(grid_rows, (32, 16, 8))
    tq = rows_per_step * GRID_W
    nq = seq // tq
    qb0, sb0 = tok0 // tq, tok0 // seq
    vmem = (2 * (2 * seq * HEAD_DIM * 2 + 2 * tq * HEAD_DIM * 2
                 + _C_BIAS_ROWS * GRID_W * 2 * GRID_W * 4)
            + 12 * _C_ROWS_PER_STAGE * GRID_W * _C_KEYS * 4)
    return pl.pallas_call(
        functools.partial(_attn_c_kernel, grid_rows=grid_rows, rows_per_step=rows_per_step),
        out_shape=jax.ShapeDtypeStruct(mixed.shape, mixed.dtype),
        grid=(nseq, HEADS, nq),
        in_specs=[
            pl.BlockSpec((tq, HEAD_DIM), lambda b, h, i: (qb0 + b * nq + i, C_Q + h)),
            pl.BlockSpec((seq, HEAD_DIM), lambda b, h, i: (sb0 + b, C_K + h)),
            pl.BlockSpec((seq, HEAD_DIM), lambda b, h, i: (sb0 + b, C_V + h)),
            pl.BlockSpec((None, None, _C_BIAS_ROWS, GRID_W, 2 * GRID_W),
                         lambda b, h, i: (l, h, 0, 0, 0)),
            pl.BlockSpec(memory_space=pl.ANY),
        ],
        out_specs=pl.BlockSpec((tq, HEAD_DIM),
                               lambda b, h, i: (qb0 + b * nq + i, _MIX_C // HEAD_DIM + h)),
        input_output_aliases={4: 0},
        scratch_shapes=_logits_scratch(_C_ROWS_PER_STAGE * GRID_W, _C_KEYS),
        compiler_params=_cparams(("parallel", "parallel", "arbitrary"), vmem),
        name="attn_neighbourhood",
    )(qkv, qkv, qkv, bias, mixed)


def _rope_tables(seqs):
    pos = np.concatenate([np.tile(np.arange(s), n) for s, n in seqs]).astype(np.int32)
    pos = jnp.asarray(pos)
    row = (pos // GRID_W).astype(F32)[:, None]
    col = (pos % GRID_W).astype(F32)[:, None]
    t = pos.astype(F32)[:, None]
    inv64 = ROPE_THETA ** (-jnp.arange(0, SUB_DIM, 2, dtype=F32) / SUB_DIM)[None, :]
    inv128 = ROPE_THETA ** (-jnp.arange(0, HEAD_DIM, 2, dtype=F32) / HEAD_DIM)[None, :]

    def pair(ang):
        c, s = jnp.cos(ang), jnp.sin(ang)
        return jnp.concatenate([c, c], axis=1), jnp.concatenate([-s, s], axis=1)

    cr, sr = pair(row * inv64)
    cc, sc = pair(col * inv64)
    cb, sb = pair(t * inv128)
    cd, sd = pair(t * inv64)
    return jnp.concatenate([cr, cc, sr, sc, cb, sb, cd, cd, sd, sd], axis=1)


def _qkv_gains(p, l):
    scale = HEAD_DIM ** -0.5
    scale_d = SUB_DIM ** -0.5
    log2e = math.log2(math.e)
    ones = lambda n: jnp.ones((n * HEAD_DIM,), F32)
    tile = lambda g, n: jnp.tile(g.astype(F32), n)
    parts = [
        tile(p["q_norm_a"][l], HEADS) * (scale * log2e), tile(p["k_norm_a"][l], KV_HEADS),
        ones(KV_HEADS),
        tile(p["q_norm_b"][l], HEADS) * scale, tile(p["k_norm_b"][l], KV_HEADS), ones(KV_HEADS),
        tile(p["q_norm_c"][l], HEADS) * scale, tile(p["k_norm_c"][l], HEADS), ones(HEADS),
        tile(p["q_norm_d"][l], 2 * HEADS) * (scale_d * log2e), tile(p["k_norm_d"][l], 2 * HEADS),
        ones(HEADS),
    ]
    return jnp.concatenate(parts).reshape(1, QKV_BLOCKS * HEAD_DIM)


def _ffn(x, g, wg, wu, wd, l):
    xn = _rmsnorm(x, g)
    hidden = _gateup(xn, wg, wu, l)
    tk_candidates = tuple(m * MXU_DIM for m in (11, 8, 4, 2, 1))
    return _mm_res(hidden, wd, l, x, FFN_RESIDUAL, tn_candidates=(1024, 512, 256, 128),
                   tk_candidates=tk_candidates)


def _token_mixing(x, p, l, seqs, rope, bias, w_in, w_out):
    lambda_init = 0.8 - 0.6 * math.exp(-0.3 * l)
    xn = _rmsnorm(x, p["norm_mix"][l])
    proj = _mm(xn, w_in, l)
    qkv = _prep(proj, _qkv_gains(p, l), rope)
    vt = _values_transposed(proj)
    sink = p["sink_b"][l].astype(F32)
    lam_params = jnp.stack([p["lambda_q1"][l], p["lambda_k1"][l],
                            p["lambda_q2"][l], p["lambda_k2"][l]]).astype(F32)
    subln = p["subln_d"][l].astype(F32).reshape(1, HEAD_DIM)
    mixed = jnp.zeros((x.shape[0], 4 * MIX_Q), BF16)
    tok0 = 0
    for seq, nseq in seqs:
        kw = dict(seq=seq, nseq=nseq, tok0=tok0)
        mixed = _attn_a(qkv, vt, mixed, **kw)
        mixed = _attn_b(qkv, sink, mixed, **kw)
        mixed = _attn_c(qkv, bias, l, mixed, **kw)
        mixed = _attn_d(qkv, vt, lam_params, subln, mixed, lambda_init=lambda_init, **kw)
        tok0 += seq * nseq
    return _mm_res(mixed, w_out, l, x, 1.0, tn_candidates=(512, 256, 128),
                   tk_candidates=(mixed.shape[1],))


def kernel(x_prompt, x_sample, norm_ffn1, w_ffn1_gate, w_ffn1_up, w_ffn1_down, norm_mix, w_in,
           q_norm_a, k_norm_a, q_norm_b, k_norm_b, sink_b, q_norm_c, k_norm_c, rpb_c,
           q_norm_d, k_norm_d, lambda_q1, lambda_k1, lambda_q2, lambda_k2, subln_d, w_out,
           norm_ffn2, w_ffn2_gate, w_ffn2_up, w_ffn2_down):
    p = dict(norm_mix=norm_mix, q_norm_a=q_norm_a, k_norm_a=k_norm_a, q_norm_b=q_norm_b,
             k_norm_b=k_norm_b, sink_b=sink_b, q_norm_c=q_norm_c, k_norm_c=k_norm_c, rpb_c=rpb_c,
             q_norm_d=q_norm_d, k_norm_d=k_norm_d, lambda_q1=lambda_q1, lambda_k1=lambda_k1,
             lambda_q2=lambda_q2, lambda_k2=lambda_k2, subln_d=subln_d)
    depth = w_in.shape[0]
    d_model = x_prompt.shape[-1]
    bp, sp, _ = x_prompt.shape
    bs, ss, _ = x_sample.shape
    seqs = ((sp, bp), (ss, bs))
    assert (bp * sp) % ss == 0
    x = jnp.concatenate([x_prompt.reshape(bp * sp, d_model),
                         x_sample.reshape(bs * ss, d_model)], axis=0)
    rope = _rope_tables(seqs)
    bias = _neighbourhood_bias(rpb_c)
    ffn1 = (w_ffn1_gate, w_ffn1_up, w_ffn1_down.astype(BF16))
    ffn2 = (w_ffn2_gate, w_ffn2_up, w_ffn2_down.astype(BF16))
    for l in range(depth):
        x = _ffn(x, norm_ffn1[l], *ffn1, l)
        x = _token_mixing(x, p, l, seqs, rope, bias, w_in, w_out)
        x = _ffn(x, norm_ffn2[l], *ffn2, l)
    y_prompt = x[:bp * sp].reshape(bp, sp, d_model)
    y_sample = x[bp * sp:].reshape(bs, ss, d_model)
    return (y_prompt, y_sample)
```
